```python
import jax, jax.numpy as jnp
from jax import lax
import numpy as np

D_MODEL = 2048
BATCH = 8
SEQ = 2048
DEPTH = 1
DEC_BATCH = 1
DEC_SEQ = 16384
PAST_LEN = 128

GRID_W = 64
NA_HEADS = 16
NA_HEAD_DIM = 64
D_ATTN = NA_HEADS * NA_HEAD_DIM
WIN_ROWS_MAX = 8
WIN_COLS = 16
POOL_WINDOWS = (2, 4, 8, 16)
POOL_GROUPS = len(POOL_WINDOWS)
D_POOL = 1024
POOL_GROUP_DIM = D_POOL // POOL_GROUPS
N_BRANCH = 2
D_IN = 3 * D_ATTN + D_POOL + N_BRANCH * D_MODEL
D_FF = 5632
RMS_EPS = 1e-6

kernel_name = "hybrid_natten_pool_macaron_encoder"


def rmsnorm(x, g):
    xf = x.astype(jnp.float32)
    y = xf * lax.rsqrt(jnp.mean(xf * xf, axis=-1, keepdims=True) + RMS_EPS)
    return (y * g.astype(jnp.float32)).astype(x.dtype)


def swiglu(x, w_gate, w_up, w_down):
    return (jax.nn.silu(x @ w_gate) * (x @ w_up)) @ w_down


def neighbourhood_attention(q, k, v, rpb):
    B, T, H, dh = q.shape
    rows = T // GRID_W
    kr = min(WIN_ROWS_MAX, rows)
    qg = q.reshape(B, rows, GRID_W, H, dh)
    kg = k.reshape(B, rows, GRID_W, H, dh)
    vg = v.reshape(B, rows, GRID_W, H, dh)
    row_start = jnp.clip(jnp.arange(rows) - kr // 2, 0, rows - kr)
    col_q = jnp.arange(GRID_W)
    col_start = jnp.clip(col_q - WIN_COLS // 2, 0, GRID_W - WIN_COLS)
    col_idx = col_start[:, None] + jnp.arange(WIN_COLS)[None, :]
    dc = col_idx - col_q[:, None] + (WIN_COLS - 1)
    scale = dh ** -0.5

    def one_row(r):
        rs = row_start[r]
        q_r = lax.dynamic_index_in_dim(qg, r, axis=1, keepdims=False)
        k_band = lax.dynamic_slice_in_dim(kg, rs, kr, axis=1)
        v_band = lax.dynamic_slice_in_dim(vg, rs, kr, axis=1)
        k_win = k_band[:, :, col_idx]
        v_win = v_band[:, :, col_idx]
        s = jnp.einsum('bwhd,brwkhd->bhwrk', q_r, k_win).astype(jnp.float32) * scale
        dr = rs + jnp.arange(kr) - r + (WIN_ROWS_MAX - 1)
        bias = rpb[:, dr[None, :, None], dc[:, None, :]]
        s = s + bias.astype(jnp.float32)[None]
        p = jax.nn.softmax(s.reshape(B, H, GRID_W, kr * WIN_COLS), axis=-1)
        p = p.reshape(B, H, GRID_W, kr, WIN_COLS).astype(v.dtype)
        return jnp.einsum('bhwrk,brwkhd->bwhd', p, v_win)

    out = lax.map(one_row, jnp.arange(rows))
    return jnp.transpose(out, (1, 0, 2, 3, 4)).reshape(B, T, H * dh)


def multiscale_pool(z, w_pool, pool_scale):
    B, T, _ = z.shape
    zf = z.astype(jnp.float32)
    cs = jnp.concatenate([jnp.zeros((B, 1, D_POOL), jnp.float32), jnp.cumsum(zf, axis=1)], axis=1)
    t = jnp.arange(T)
    outs = []
    for gi, w in enumerate(POOL_WINDOWS):
        lo = jnp.clip(t - w // 2, 0, T)
        hi = jnp.clip(t + w // 2, 0, T)
        sl = slice(gi * POOL_GROUP_DIM, (gi + 1) * POOL_GROUP_DIM)
        csg = cs[:, :, sl]
        cnt = (hi - lo).astype(jnp.float32)[None, :, None]
        outs.append((csg[:, hi] - csg[:, lo]) / cnt - zf[:, :, sl])
    pooled = jnp.stack(outs, axis=2).astype(z.dtype)
    mixed = jnp.einsum('btgc,gcd->btgd', pooled, w_pool)
    return mixed.reshape(B, T, D_POOL) * pool_scale


def hybrid_mixer(u, w_in, rpb, w_pool, pool_scale, w_branch_attn, w_branch_pool, w_out):
    B, T, _ = u.shape
    proj = u @ w_in
    q, k, v, zp, g = jnp.split(proj, [D_ATTN, 2 * D_ATTN, 3 * D_ATTN, 3 * D_ATTN + D_POOL], axis=-1)
    heads = lambda a: a.reshape(B, T, NA_HEADS, NA_HEAD_DIM)
    y_attn = neighbourhood_attention(heads(q), heads(k), heads(v), rpb) @ w_branch_attn
    y_pool = multiscale_pool(zp, w_pool, pool_scale) @ w_branch_pool
    g_a, g_p = jnp.split(jax.nn.sigmoid(g), N_BRANCH, axis=-1)
    return (g_a * y_attn + g_p * y_pool) @ w_out


def encoder_trunk(x, g_ffn1, w1_gate, w1_up, w1_down, g_mix, w_in, rpb, w_pool, pool_scale,
                  w_branch_attn, w_branch_pool, w_out, g_ffn2, w2_gate, w2_up, w2_down, g_final):
    h = x
    for l in range(DEPTH):
        h = h + 0.5 * swiglu(rmsnorm(h, g_ffn1[l]), w1_gate[l], w1_up[l], w1_down[l])
        h = h + hybrid_mixer(rmsnorm(h, g_mix[l]), w_in[l], rpb[l], w_pool[l], pool_scale[l],
                             w_branch_attn[l], w_branch_pool[l], w_out[l])
        h = h + 0.5 * swiglu(rmsnorm(h, g_ffn2[l]), w2_gate[l], w2_up[l], w2_down[l])
    return rmsnorm(h, g_final)


def setup_inputs(seed: int = 0) -> dict:
    key = jax.random.key(seed)
    ks = jax.random.split(key, 20)
    f32 = jnp.float32
    nrm = lambda k, shape, s: jax.random.normal(k, shape, f32) * s
    L = DEPTH
    return {
        "x_prompt": nrm(ks[0], (BATCH, SEQ, D_MODEL), 1.0),
        "x_sample": nrm(ks[1], (DEC_BATCH, DEC_SEQ, D_MODEL), 1.0),
        "g_ffn1": 1.0 + nrm(ks[2], (L, D_MODEL), 0.02),
        "w1_gate": nrm(ks[3], (L, D_MODEL, D_FF), D_MODEL ** -0.5),
        "w1_up": nrm(ks[4], (L, D_MODEL, D_FF), D_MODEL ** -0.5),
        "w1_down": nrm(ks[5], (L, D_FF, D_MODEL), D_FF ** -0.5),
        "g_mix": 1.0 + nrm(ks[6], (L, D_MODEL), 0.02),
        "w_in": nrm(ks[7], (L, D_MODEL, D_IN), D_MODEL ** -0.5),
        "rpb": nrm(ks[8], (L, NA_HEADS, 2 * WIN_ROWS_MAX - 1, 2 * WIN_COLS - 1), 0.02),
        "w_pool": nrm(ks[9], (L, POOL_GROUPS, POOL_GROUP_DIM, POOL_GROUP_DIM), POOL_GROUP_DIM ** -0.5),
        "pool_scale": 1.0 + nrm(ks[10], (L, D_POOL), 0.02),
        "w_branch_attn": nrm(ks[11], (L, D_ATTN, D_MODEL), D_ATTN ** -0.5),
        "w_branch_pool": nrm(ks[12], (L, D_POOL, D_MODEL), D_POOL ** -0.5),
        "w_out": nrm(ks[13], (L, D_MODEL, D_MODEL), D_MODEL ** -0.5),
        "g_ffn2": 1.0 + nrm(ks[14], (L, D_MODEL), 0.02),
        "w2_gate": nrm(ks[15], (L, D_MODEL, D_FF), D_MODEL ** -0.5),
        "w2_up": nrm(ks[16], (L, D_MODEL, D_FF), D_MODEL ** -0.5),
        "w2_down": nrm(ks[17], (L, D_FF, D_MODEL), D_FF ** -0.5),
        "g_final": 1.0 + nrm(ks[18], (D_MODEL,), 0.02),
    }


def reference(x_prompt, x_sample, g_ffn1, w1_gate, w1_up, w1_down, g_mix, w_in, rpb, w_pool,
              pool_scale, w_branch_attn, w_branch_pool, w_out, g_ffn2, w2_gate, w2_up, w2_down, g_final):
    y_prompt = encoder_trunk(x_prompt, g_ffn1, w1_gate, w1_up, w1_down, g_mix, w_in, rpb, w_pool,
                             pool_scale, w_branch_attn, w_branch_pool, w_out, g_ffn2, w2_gate, w2_up,
                             w2_down, g_final)
    y_sample = encoder_trunk(x_sample, g_ffn1, w1_gate, w1_up, w1_down, g_mix, w_in, rpb, w_pool,
                             pool_scale, w_branch_attn, w_branch_pool, w_out, g_ffn2, w2_gate, w2_up,
                             w2_down, g_final)
    return (y_prompt, y_sample)
```

```python
import functools

import jax
import jax.numpy as jnp
from jax import lax
from jax.experimental import pallas as pl
from jax.experimental.pallas import tpu as pltpu

F32 = jnp.float32
BF16 = jnp.bfloat16

GRID_W = 64
NA_HEADS = 16
NA_HEAD_DIM = 64
D_ATTN = NA_HEADS * NA_HEAD_DIM
WIN_ROWS = 8
WIN_COLS = 16
POOL_WINDOWS = (2, 4, 8, 16)
POOL_HALO = max(POOL_WINDOWS) // 2
RMS_EPS = 1e-6
MASK_BIAS = -1e30

V7X_LANES = 128
V7X_VMEM_LIMIT_BYTES = 56 * 1024 * 1024

FFN_TOKENS = 512
FFN_HIDDEN = 512
PROJ_TOKENS = 512
PROJ_COLS = 1024
ATTN_ROWS = 8
ATTN_KV_BLOCK_ROWS = 4
ATTN_KV_BLOCKS = 4
ATTN_WIN_ROWS = ATTN_KV_BLOCK_ROWS * ATTN_KV_BLOCKS
MERGE_TOKENS = 256
HEAD_PAIRS = NA_HEADS // 2


def _params(semantics):
    return pltpu.CompilerParams(dimension_semantics=semantics, vmem_limit_bytes=V7X_VMEM_LIMIT_BYTES)


def _resident(shape):
    zeros = (0,) * len(shape)
    return pl.BlockSpec(shape, lambda *_: zeros, pipeline_mode=pl.Buffered(1))


def _rmsnorm(x, g):
    return x * lax.rsqrt(jnp.mean(x * x, axis=-1, keepdims=True) + RMS_EPS) * g


def _sigmoid(x):
    return 1.0 / (1.0 + jnp.exp(-x))


def _ffn_body(x_ref, gpre_ref, wg_ref, wu_ref, wd_ref, gpost_ref, *rest, final):
    if final:
        y_ref, xn_ref, acc_ref = rest
    else:
        h_ref, u_ref, xn_ref, acc_ref = rest
    f = pl.program_id(1)

    @pl.when(f == 0)
    def _():
        xn_ref[...] = _rmsnorm(x_ref[...], gpre_ref[...]).astype(BF16)
        acc_ref[...] = jnp.zeros_like(acc_ref)

    xn = xn_ref[...]
    gate = jnp.dot(xn, wg_ref[...], preferred_element_type=F32)
    up = jnp.dot(xn, wu_ref[...], preferred_element_type=F32)
    hid = (gate * _sigmoid(gate) * up).astype(BF16)
    acc_ref[...] += jnp.dot(hid, wd_ref[...], preferred_element_type=F32)

    @pl.when(f == pl.num_programs(1) - 1)
    def _():
        h = x_ref[...] + 0.5 * acc_ref[...]
        if final:
            y_ref[...] = _rmsnorm(h, gpost_ref[...])
        else:
            h_ref[...] = h
            u_ref[...] = _rmsnorm(h, gpost_ref[...]).astype(BF16)


def _ffn(x, g_pre, wg, wu, wd, g_post, *, final):
    T, D = x.shape
    Fh = wg.shape[1]
    tm, tf = FFN_TOKENS, FFN_HIDDEN
    row = pl.BlockSpec((tm, D), lambda i, f: (i, 0))
    vec = pl.BlockSpec((1, D), lambda i, f: (0, 0))
    if final:
        out_shape = jax.ShapeDtypeStruct((T, D), F32)
        out_specs = row
    else:
        out_shape = (jax.ShapeDtypeStruct((T, D), F32), jax.ShapeDtypeStruct((T, D), BF16))
        out_specs = (row, row)
    return pl.pallas_call(
        functools.partial(_ffn_body, final=final),
        out_shape=out_shape,
        grid=(T // tm, Fh // tf),
        in_specs=[
            row, vec,
            pl.BlockSpec((D, tf), lambda i, f: (0, f)),
            pl.BlockSpec((D, tf), lambda i, f: (0, f)),
            pl.BlockSpec((tf, D), lambda i, f: (f, 0)),
            vec,
        ],
        out_specs=out_specs,
        scratch_shapes=[pltpu.VMEM((tm, D), BF16), pltpu.VMEM((tm, D), F32)],
        compiler_params=_params(("parallel", "arbitrary")),
        name="ffn_final" if final else "ffn_mid",
    )(x, g_pre, wg, wu, wd, g_post)


def _inproj_body(u_ref, w_ref, q_ref, k_ref, v_ref, z_ref, gate_ref):
    n = pl.program_id(1)
    acc = jnp.dot(u_ref[...], w_ref[...], preferred_element_type=F32)

    @pl.when(n == 0)
    def _():
        q_ref[...] = (acc * (NA_HEAD_DIM ** -0.5)).astype(BF16)

    @pl.when(n == 1)
    def _():
        k_ref[...] = acc.astype(BF16)

    @pl.when(n == 2)
    def _():
        v_ref[...] = acc.astype(BF16)

    @pl.when(n == 3)
    def _():
        z_ref[...] = acc

    @pl.when(n >= 4)
    def _():
        gate_ref[...] = _sigmoid(acc)


def _inproj(u, w_in):
    T, D = u.shape
    d_in = w_in.shape[1]
    tm, tn = PROJ_TOKENS, PROJ_COLS
    n_head_blocks = 4
    d_gate = d_in - n_head_blocks * tn
    once = pl.BlockSpec((tm, tn), lambda i, n: (i, 0))
    return pl.pallas_call(
        _inproj_body,
        out_shape=(
            jax.ShapeDtypeStruct((T, tn), BF16),
            jax.ShapeDtypeStruct((T, tn), BF16),
            jax.ShapeDtypeStruct((T, tn), BF16),
            jax.ShapeDtypeStruct((T, tn), F32),
            jax.ShapeDtypeStruct((T, d_gate), F32),
        ),
        grid=(T // tm, d_in // tn),
        in_specs=[
            pl.BlockSpec((tm, D), lambda i, n: (i, 0)),
            pl.BlockSpec((D, tn), lambda i, n: (0, n)),
        ],
        out_specs=(
            once, once, once, once,
            pl.BlockSpec((tm, tn), lambda i, n: (i, jnp.maximum(n - n_head_blocks, 0))),
        ),
        compiler_params=_params(("parallel", "arbitrary")),
        name="in_proj",
    )(u, w_in)


def _attn_bias_table(rpb):
    c = jnp.arange(GRID_W)
    cs = jnp.clip(c - WIN_COLS // 2, 0, GRID_W - WIN_COLS)
    kc = jnp.arange(GRID_W)
    valid = (kc[None, :] >= cs[:, None]) & (kc[None, :] < cs[:, None] + WIN_COLS)
    dc = jnp.clip(kc[None, :] - c[:, None] + (WIN_COLS - 1), 0, 2 * WIN_COLS - 2)
    toeplitz = jnp.where(valid[None, None], rpb[:, :, dc].astype(F32), MASK_BIAS)
    rows = jnp.arange(WIN_ROWS)[:, None] + jnp.arange(WIN_ROWS)[None, :]
    band = toeplitz[:, rows]
    band = band.transpose(1, 0, 3, 2, 4).reshape(WIN_ROWS, NA_HEADS, GRID_W, WIN_ROWS * GRID_W)
    return band.reshape(WIN_ROWS, HEAD_PAIRS, 2 * GRID_W, WIN_ROWS * GRID_W)


def _attn_window_start(step, img_rows):
    row0 = step * ATTN_ROWS
    img0 = (row0 // img_rows) * img_rows
    r0 = row0 - img0
    ws = jnp.clip(r0 - WIN_ROWS // 2, 0, img_rows - ATTN_WIN_ROWS)
    return img0 + ws, r0, ws


def _attn_body(q_ref, *rest, img_rows):
    k_blocks = rest[:ATTN_KV_BLOCKS]
    v_blocks = rest[ATTN_KV_BLOCKS:2 * ATTN_KV_BLOCKS]
    bias_ref, o_ref, kw_ref, vw_ref = rest[2 * ATTN_KV_BLOCKS:]
    blk = ATTN_KV_BLOCK_ROWS * GRID_W
    for j in range(ATTN_KV_BLOCKS):
        kw_ref[j * blk:(j + 1) * blk, :] = k_blocks[j][...]
        vw_ref[j * blk:(j + 1) * blk, :] = v_blocks[j][...]

    _, r0, ws = _attn_window_start(pl.program_id(0), img_rows)
    lane = lax.broadcasted_iota(jnp.int32, (GRID_W, V7X_LANES), 1)
    first_head = lane < NA_HEAD_DIM
    band = WIN_ROWS * GRID_W

    def row_body(i, carry):
        r = r0 + i
        rs = jnp.clip(r - WIN_ROWS // 2, 0, img_rows - WIN_ROWS)
        key0 = pl.multiple_of((rs - ws) * GRID_W, GRID_W)
        off = rs - r + (WIN_ROWS - 1)
        q0 = pl.multiple_of(i * GRID_W, GRID_W)
        for p in range(HEAD_PAIRS):
            lanes = slice(p * V7X_LANES, (p + 1) * V7X_LANES)
            qp = q_ref[pl.ds(q0, GRID_W), lanes]
            zero = jnp.zeros_like(qp)
            lhs = jnp.concatenate([jnp.where(first_head, qp, zero), jnp.where(first_head, zero, qp)], axis=0)
            kb = kw_ref[pl.ds(key0, band), lanes]
            s = lax.dot_general(lhs, kb, (((1,), (1,)), ((), ())), preferred_element_type=F32)
            s = s + bias_ref[off, p]
            m = jnp.max(s, axis=-1, keepdims=True)
            e = jnp.exp(s - m)
            inv = 1.0 / jnp.sum(e, axis=-1, keepdims=True)
            vb = vw_ref[pl.ds(key0, band), lanes]
            o = jnp.dot(e.astype(BF16), vb, preferred_element_type=F32) * inv
            o_ref[pl.ds(q0, GRID_W), lanes] = jnp.where(first_head, o[:GRID_W], o[GRID_W:]).astype(BF16)
        return carry

    lax.fori_loop(0, ATTN_ROWS, row_body, 0)


def _attention(q, k, v, bias, *, img_rows):
    T, C = q.shape
    tq = ATTN_ROWS * GRID_W
    blk = ATTN_KV_BLOCK_ROWS * GRID_W

    def kv_spec(j):
        def index(s):
            win0, _, _ = _attn_window_start(s, img_rows)
            return (win0 // ATTN_KV_BLOCK_ROWS + j, 0)
        return pl.BlockSpec((blk, C), index)

    kv_specs = [kv_spec(j) for j in range(ATTN_KV_BLOCKS)]
    return pl.pallas_call(
        functools.partial(_attn_body, img_rows=img_rows),
        out_shape=jax.ShapeDtypeStruct((T, C), BF16),
        grid=(T // tq,),
        in_specs=[pl.BlockSpec((tq, C), lambda s: (s, 0))] + kv_specs + kv_specs + [_resident(bias.shape)],
        out_specs=pl.BlockSpec((tq, C), lambda s: (s, 0)),
        scratch_shapes=[pltpu.VMEM((ATTN_WIN_ROWS * GRID_W, C), BF16)] * 2,
        compiler_params=_params(("parallel",)),
        name="nbr_attention",
    )(q, *([k] * ATTN_KV_BLOCKS), *([v] * ATTN_KV_BLOCKS), bias)


def _merge_body(attn_ref, z_ref, zprev_ref, znext_ref, ga_ref, gp_ref, h_ref, wpool_ref, scale_ref,
                wba_ref, wbp_ref, wout_ref, o_ref, ext_ref, mix_ref, *, seq_len):
    tm = z_ref.shape[0]
    gd = wpool_ref.shape[1]
    t0 = (pl.program_id(0) * tm) % seq_len
    ext_ref[0:POOL_HALO, :] = jnp.where(t0 == 0, 0.0, zprev_ref[...])
    ext_ref[POOL_HALO:POOL_HALO + tm, :] = z_ref[...]
    ext_ref[POOL_HALO + tm:, :] = jnp.where(t0 + tm == seq_len, 0.0, znext_ref[...])
    tpos = t0 + lax.broadcasted_iota(jnp.int32, (tm, 1), 0)
    for g, w in enumerate(POOL_WINDOWS):
        cols = slice(g * gd, (g + 1) * gd)
        total = ext_ref[pl.ds(POOL_HALO - w // 2, tm), cols]
        for d in range(-w // 2 + 1, w // 2):
            total = total + ext_ref[pl.ds(POOL_HALO + d, tm), cols]
        cnt = jnp.minimum(tpos + w // 2, seq_len) - jnp.maximum(tpos - w // 2, 0)
        pooled = total / cnt.astype(F32) - z_ref[:, cols]
        mixed = jnp.dot(pooled.astype(BF16), wpool_ref[g], preferred_element_type=F32)
        mix_ref[:, cols] = (mixed * scale_ref[:, cols]).astype(BF16)
    y_pool = jnp.dot(mix_ref[...], wbp_ref[...], preferred_element_type=F32)
    y_attn = jnp.dot(attn_ref[...], wba_ref[...], preferred_element_type=F32)
    merged = (ga_ref[...] * y_attn + gp_ref[...] * y_pool).astype(BF16)
    o_ref[...] = h_ref[...] + jnp.dot(merged, wout_ref[...], preferred_element_type=F32)


def _merge(attn, z, gates, h, w_pool, pool_scale, w_ba, w_bp, w_out, *, seq_len):
    T, D = h.shape
    C = z.shape[1]
    tm = MERGE_TOKENS
    halo_blocks = tm // POOL_HALO
    last_halo = T // POOL_HALO - 1
    return pl.pallas_call(
        functools.partial(_merge_body, seq_len=seq_len),
        out_shape=jax.ShapeDtypeStruct((T, D), F32),
        grid=(T // tm,),
        in_specs=[
            pl.BlockSpec((tm, attn.shape[1]), lambda i: (i, 0)),
            pl.BlockSpec((tm, C), lambda i: (i, 0)),
            pl.BlockSpec((POOL_HALO, C), lambda i: (jnp.maximum(i * halo_blocks - 1, 0), 0)),
            pl.BlockSpec((POOL_HALO, C), lambda i: (jnp.minimum((i + 1) * halo_blocks, last_halo), 0)),
            pl.BlockSpec((tm, D), lambda i: (i, 0)),
            pl.BlockSpec((tm, D), lambda i: (i, 1)),
            pl.BlockSpec((tm, D), lambda i: (i, 0)),
            _resident(w_pool.shape), _resident(pool_scale.shape),
            _resident(w_ba.shape), _resident(w_bp.shape), _resident(w_out.shape),
        ],
        out_specs=pl.BlockSpec((tm, D), lambda i: (i, 0)),
        scratch_shapes=[pltpu.VMEM((tm + 2 * POOL_HALO, C), F32), pltpu.VMEM((tm, C), BF16)],
        compiler_params=_params(("parallel",)),
        name="pool_merge_out",
    )(attn, z, z, z, gates, gates, h, w_pool, pool_scale, w_ba, w_bp, w_out)


def _trunk(x, w, *, seq_len):
    B, S, D = x.shape
    x2 = x.reshape(B * S, D)
    h1, u = _ffn(x2, w["g_ffn1"], w["w1_gate"], w["w1_up"], w["w1_down"], w["g_mix"], final=False)
    q, k, v, z, gates = _inproj(u, w["w_in"])
    attn = _attention(q, k, v, w["attn_bias"], img_rows=seq_len // GRID_W)
    h2 = _merge(attn, z, gates, h1, w["w_pool"], w["pool_scale"], w["w_branch_attn"], w["w_branch_pool"],
                w["w_out"], seq_len=seq_len)
    y = _ffn(h2, w["g_ffn2"], w["w2_gate"], w["w2_up"], w["w2_down"], w["g_final"], final=True)
    return y.reshape(B, S, D)


def kernel(x_prompt, x_sample, g_ffn1, w1_gate, w1_up, w1_down, g_mix, w_in, rpb, w_pool, pool_scale,
           w_branch_attn, w_branch_pool, w_out, g_ffn2, w2_gate, w2_up, w2_down, g_final):
    assert g_ffn1.shape[0] == 1, "single-layer block"
    bf = lambda a: a[0].astype(BF16)
    vec = lambda a: a.reshape(1, -1).astype(F32)
    w = {
        "g_ffn1": vec(g_ffn1), "w1_gate": bf(w1_gate), "w1_up": bf(w1_up), "w1_down": bf(w1_down),
        "g_mix": vec(g_mix), "w_in": bf(w_in), "attn_bias": _attn_bias_table(rpb[0]),
        "w_pool": bf(w_pool), "pool_scale": vec(pool_scale),
        "w_branch_attn": bf(w_branch_attn), "w_branch_pool": bf(w_branch_pool), "w_out": bf(w_out),
        "g_ffn2": vec(g_ffn2), "w2_gate": bf(w2_gate), "w2_up": bf(w2_up), "w2_down": bf(w2_down),
        "g_final": vec(g_final),
    }
    y_prompt = _trunk(x_prompt, w, seq_len=x_prompt.shape[1])
    y_sample = _trunk(x_sample, w, seq_len=x_sample.shape[1])
    return (y_prompt, y_sample)
```

```python
import functools

import jax
import jax.numpy as jnp
from jax import lax
from jax.experimental import pallas as pl
from jax.experimental.pallas import tpu as pltpu

F32 = jnp.float32
BF16 = jnp.bfloat16

GRID_W = 64
NA_HEADS = 16
NA_HEAD_DIM = 64
D_ATTN = NA_HEADS * NA_HEAD_DIM
WIN_ROWS = 8
WIN_COLS = 16
POOL_WINDOWS = (2, 4, 8, 16)
POOL_HALO = max(POOL_WINDOWS) // 2
RMS_EPS = 1e-6
MASK_BIAS = -1e30

V7X_LANES = 128
V7X_VMEM_LIMIT_BYTES = 56 * 1024 * 1024

FFN_TOKENS = 1024
FFN_ROW_GROUP = 512
FFN_HIDDEN = 512
PROJ_TOKENS = 1024
PROJ_COLS = 1024
ATTN_ROWS = 8
ATTN_KV_BLOCK_ROWS = 4
ATTN_KV_BLOCKS = 4
ATTN_WIN_ROWS = ATTN_KV_BLOCK_ROWS * ATTN_KV_BLOCKS
MERGE_TOKENS = 256
HEAD_PAIRS = NA_HEADS // 2


def _params(semantics):
    return pltpu.CompilerParams(dimension_semantics=semantics, vmem_limit_bytes=V7X_VMEM_LIMIT_BYTES)


def _resident(shape):
    zeros = (0,) * len(shape)
    return pl.BlockSpec(shape, lambda *_: zeros, pipeline_mode=pl.Buffered(1))


def _rmsnorm(x, g):
    return x * lax.rsqrt(jnp.mean(x * x, axis=-1, keepdims=True) + RMS_EPS) * g


def _sigmoid(x):
    return 1.0 / (1.0 + jnp.exp(-x))


def _ffn_body(x_hbm, gpre_ref, wg_ref, wu_ref, wd_ref, gpost_ref, *rest, final):
    if final:
        acc_ref, xbuf, sem, xn_ref = rest
    else:
        acc_ref, u_ref, xbuf, sem = rest
        xn_ref = u_ref
    i, f = pl.program_id(0), pl.program_id(1)
    n_tiles, n_chunks = pl.num_programs(0), pl.num_programs(1)
    tm = xbuf.shape[0]

    def x_copy(tile):
        return pltpu.make_async_copy(x_hbm.at[pl.ds(tile * tm, tm)], xbuf, sem)

    def rows_step(rows, first, last):
        if first:
            xn = _rmsnorm(xbuf[rows, :], gpre_ref[...]).astype(BF16)
            xn_ref[rows, :] = xn
        else:
            xn = xn_ref[rows, :]
        gate = jnp.dot(xn, wg_ref[...], preferred_element_type=F32)
        up = jnp.dot(xn, wu_ref[...], preferred_element_type=F32)
        hid = (0.5 * gate * _sigmoid(gate) * up).astype(BF16)
        acc = (xbuf[rows, :] if first else acc_ref[rows, :]) + jnp.dot(hid, wd_ref[...], preferred_element_type=F32)
        if not last:
            acc_ref[rows, :] = acc
        elif final:
            acc_ref[rows, :] = _rmsnorm(acc, gpost_ref[...])
        else:
            acc_ref[rows, :] = acc
            u_ref[rows, :] = _rmsnorm(acc, gpost_ref[...]).astype(BF16)

    def tile_step(first, last):
        for r in range(tm // FFN_ROW_GROUP):
            rows_step(pl.ds(r * FFN_ROW_GROUP, FFN_ROW_GROUP), first, last)

    @pl.when(jnp.logical_and(i == 0, f == 0))
    def _():
        x_copy(0).start()

    @pl.when(f == 0)
    def _():
        x_copy(i).wait()
        tile_step(True, False)

    @pl.when(jnp.logical_and(f == 1, i + 1 < n_tiles))
    def _():
        x_copy(i + 1).start()

    @pl.when(jnp.logical_and(f > 0, f < n_chunks - 1))
    def _():
        tile_step(False, False)

    @pl.when(f == n_chunks - 1)
    def _():
        tile_step(False, True)


def _ffn(x, g_pre, wg, wu, wd, g_post, *, final):
    T, D = x.shape
    Fh = wg.shape[1]
    tm, tf = FFN_TOKENS, FFN_HIDDEN
    assert Fh // tf >= 2 and T % tm == 0 and Fh % tf == 0
    row = pl.BlockSpec((tm, D), lambda i, f: (i, 0))
    vec = pl.BlockSpec((1, D), lambda i, f: (0, 0))
    scratch = [pltpu.VMEM((tm, D), F32), pltpu.SemaphoreType.DMA(())]
    if final:
        out_shape = jax.ShapeDtypeStruct((T, D), F32)
        out_specs = row
        scratch.append(pltpu.VMEM((tm, D), BF16))
    else:
        out_shape = (jax.ShapeDtypeStruct((T, D), F32), jax.ShapeDtypeStruct((T, D), BF16))
        out_specs = (row, row)
    return pl.pallas_call(
        functools.partial(_ffn_body, final=final),
        out_shape=out_shape,
        grid=(T // tm, Fh // tf),
        in_specs=[
            pl.BlockSpec(memory_space=pl.ANY), vec,
            pl.BlockSpec((D, tf), lambda i, f: (0, f)),
            pl.BlockSpec((D, tf), lambda i, f: (0, f)),
            pl.BlockSpec((tf, D), lambda i, f: (f, 0)),
            vec,
        ],
        out_specs=out_specs,
        scratch_shapes=scratch,
        compiler_params=_params(("arbitrary", "arbitrary")),
        name="ffn_final" if final else "ffn_mid",
    )(x, g_pre, wg, wu, wd, g_post)


def _inproj_body(u_ref, w_ref, q_ref, k_ref, v_ref, z_ref, gate_ref):
    n = pl.program_id(1)

    def proj():
        return jnp.dot(u_ref[...], w_ref[...], preferred_element_type=F32)

    @pl.when(n == 0)
    def _():
        q_ref[...] = (proj() * (NA_HEAD_DIM ** -0.5)).astype(BF16)

    @pl.when(n == 1)
    def _():
        k_ref[...] = proj().astype(BF16)

    @pl.when(n == 2)
    def _():
        v_ref[...] = proj().astype(BF16)

    @pl.when(n == 3)
    def _():
        z_ref[...] = proj()

    @pl.when(n >= 4)
    def _():
        gate_ref[...] = _sigmoid(proj())


def _inproj(u, w_in):
    T, D = u.shape
    d_in = w_in.shape[1]
    tm, tn = PROJ_TOKENS, PROJ_COLS
    n_head_blocks = 4
    d_gate = d_in - n_head_blocks * tn
    once = pl.BlockSpec((tm, tn), lambda i, n: (i, 0))
    return pl.pallas_call(
        _inproj_body,
        out_shape=(
            jax.ShapeDtypeStruct((T, tn), BF16),
            jax.ShapeDtypeStruct((T, tn), BF16),
            jax.ShapeDtypeStruct((T, tn), BF16),
            jax.ShapeDtypeStruct((T, tn), F32),
            jax.ShapeDtypeStruct((T, d_gate), F32),
        ),
        grid=(T // tm, d_in // tn),
        in_specs=[
            pl.BlockSpec((tm, D), lambda i, n: (i, 0)),
            pl.BlockSpec((D, tn), lambda i, n: (0, n)),
        ],
        out_specs=(
            once, once, once, once,
            pl.BlockSpec((tm, tn), lambda i, n: (i, jnp.maximum(n - n_head_blocks, 0))),
        ),
        compiler_params=_params(("parallel", "arbitrary")),
        name="in_proj",
    )(u, w_in)


def _attn_bias_table(rpb):
    c = jnp.arange(GRID_W)
    cs = jnp.clip(c - WIN_COLS // 2, 0, GRID_W - WIN_COLS)
    kc = jnp.arange(GRID_W)
    valid = (kc[None, :] >= cs[:, None]) & (kc[None, :] < cs[:, None] + WIN_COLS)
    dc = jnp.clip(kc[None, :] - c[:, None] + (WIN_COLS - 1), 0, 2 * WIN_COLS - 2)
    toeplitz = jnp.where(valid[None, None], rpb[:, :, dc].astype(F32), MASK_BIAS)
    rows = jnp.arange(WIN_ROWS)[:, None] + jnp.arange(WIN_ROWS)[None, :]
    band = toeplitz[:, rows]
    band = band.transpose(1, 0, 3, 2, 4).reshape(WIN_ROWS, NA_HEADS, GRID_W, WIN_ROWS * GRID_W)
    return band.reshape(WIN_ROWS, HEAD_PAIRS, 2 * GRID_W, WIN_ROWS * GRID_W)


def _attn_window_start(step, img_rows):
    row0 = step * ATTN_ROWS
    img0 = (row0 // img_rows) * img_rows
    r0 = row0 - img0
    ws = jnp.clip(r0 - WIN_ROWS // 2, 0, img_rows - ATTN_WIN_ROWS)
    return img0 + ws, r0, ws


def _attn_body(q_ref, *rest, img_rows):
    k_blocks = rest[:ATTN_KV_BLOCKS]
    v_blocks = rest[ATTN_KV_BLOCKS:2 * ATTN_KV_BLOCKS]
    bias_ref, o_ref, kw_ref, vw_ref = rest[2 * ATTN_KV_BLOCKS:]
    blk = ATTN_KV_BLOCK_ROWS * GRID_W
    for j in range(ATTN_KV_BLOCKS):
        kw_ref[j * blk:(j + 1) * blk, :] = k_blocks[j][...]
        vw_ref[j * blk:(j + 1) * blk, :] = v_blocks[j][...]

    _, r0, ws = _attn_window_start(pl.program_id(0), img_rows)
    lane = lax.broadcasted_iota(jnp.int32, (GRID_W, V7X_LANES), 1)
    first_head = lane < NA_HEAD_DIM
    band = WIN_ROWS * GRID_W

    def scores(i, p, key0, off):
        lanes = slice(p * V7X_LANES, (p + 1) * V7X_LANES)
        qp = q_ref[i * GRID_W:(i + 1) * GRID_W, lanes]
        zero = jnp.zeros_like(qp)
        lhs = jnp.concatenate([jnp.where(first_head, qp, zero), jnp.where(first_head, zero, qp)], axis=0)
        kb = kw_ref[pl.ds(key0, band), lanes]
        s = lax.dot_general(lhs, kb, (((1,), (1,)), ((), ())), preferred_element_type=F32)
        return s + bias_ref[off, p]

    def finish(i, p, key0, s):
        lanes = slice(p * V7X_LANES, (p + 1) * V7X_LANES)
        m = jnp.max(s, axis=-1, keepdims=True)
        e = jnp.exp(s - m)
        inv = 1.0 / jnp.sum(e, axis=-1, keepdims=True)
        vb = vw_ref[pl.ds(key0, band), lanes]
        o = jnp.dot(e.astype(BF16), vb, preferred_element_type=F32) * inv
        o_ref[i * GRID_W:(i + 1) * GRID_W, lanes] = jnp.where(first_head, o[:GRID_W], o[GRID_W:]).astype(BF16)

    pending = None
    for i in range(ATTN_ROWS):
        r = r0 + i
        rs = jnp.clip(r - WIN_ROWS // 2, 0, img_rows - WIN_ROWS)
        key0 = pl.multiple_of((rs - ws) * GRID_W, GRID_W)
        off = rs - r + (WIN_ROWS - 1)
        for p in range(HEAD_PAIRS):
            s = scores(i, p, key0, off)
            if pending is not None:
                finish(*pending)
            pending = (i, p, key0, s)
    finish(*pending)


def _attention(q, k, v, bias, *, img_rows):
    T, C = q.shape
    tq = ATTN_ROWS * GRID_W
    blk = ATTN_KV_BLOCK_ROWS * GRID_W

    def kv_spec(j):
        def index(s):
            win0, _, _ = _attn_window_start(s, img_rows)
            return (win0 // ATTN_KV_BLOCK_ROWS + j, 0)
        return pl.BlockSpec((blk, C), index)

    kv_specs = [kv_spec(j) for j in range(ATTN_KV_BLOCKS)]
    return pl.pallas_call(
        functools.partial(_attn_body, img_rows=img_rows),
        out_shape=jax.ShapeDtypeStruct((T, C), BF16),
        grid=(T // tq,),
        in_specs=[pl.BlockSpec((tq, C), lambda s: (s, 0))] + kv_specs + kv_specs + [_resident(bias.shape)],
        out_specs=pl.BlockSpec((tq, C), lambda s: (s, 0)),
        scratch_shapes=[pltpu.VMEM((ATTN_WIN_ROWS * GRID_W, C), BF16)] * 2,
        compiler_params=_params(("parallel",)),
        name="nbr_attention",
    )(q, *([k] * ATTN_KV_BLOCKS), *([v] * ATTN_KV_BLOCKS), bias)


def _merge_body(attn_ref, z_ref, zprev_ref, znext_ref, ga_ref, gp_ref, h_ref, wpool_ref, scale_ref,
                wba_ref, wbp_ref, wout_ref, o_ref, ext_ref, mix_ref, *, seq_len):
    tm = z_ref.shape[0]
    gd = wpool_ref.shape[1]
    t0 = (pl.program_id(0) * tm) % seq_len
    ext_ref[0:POOL_HALO, :] = jnp.where(t0 == 0, 0.0, zprev_ref[...])
    ext_ref[POOL_HALO:POOL_HALO + tm, :] = z_ref[...]
    ext_ref[POOL_HALO + tm:, :] = jnp.where(t0 + tm == seq_len, 0.0, znext_ref[...])
    tpos = t0 + lax.broadcasted_iota(jnp.int32, (tm, 1), 0)
    for g, w in enumerate(POOL_WINDOWS):
        cols = slice(g * gd, (g + 1) * gd)
        total = ext_ref[pl.ds(POOL_HALO - w // 2, tm), cols]
        for d in range(-w // 2 + 1, w // 2):
            total = total + ext_ref[pl.ds(POOL_HALO + d, tm), cols]
        cnt = jnp.minimum(tpos + w // 2, seq_len) - jnp.maximum(tpos - w // 2, 0)
        pooled = total / cnt.astype(F32) - z_ref[:, cols]
        mixed = jnp.dot(pooled.astype(BF16), wpool_ref[g], preferred_element_type=F32)
        mix_ref[:, cols] = (mixed * scale_ref[:, cols]).astype(BF16)
    y_pool = jnp.dot(mix_ref[...], wbp_ref[...], preferred_element_type=F32)
    y_attn = jnp.dot(attn_ref[...], wba_ref[...], preferred_element_type=F32)
    merged = (ga_ref[...] * y_attn + gp_ref[...] * y_pool).astype(BF16)
    o_ref[...] = h_ref[...] + jnp.dot(merged, wout_ref[...], preferred_element_type=F32)


def _merge(attn, z, gates, h, w_pool, pool_scale, w_ba, w_bp, w_out, *, seq_len):
    T, D = h.shape
    C = z.shape[1]
    tm = MERGE_TOKENS
    halo_blocks = tm // POOL_HALO
    last_halo = T // POOL_HALO - 1
    return pl.pallas_call(
        functools.partial(_merge_body, seq_len=seq_len),
        out_shape=jax.ShapeDtypeStruct((T, D), F32),
        grid=(T // tm,),
        in_specs=[
            pl.BlockSpec((tm, attn.shape[1]), lambda i: (i, 0)),
            pl.BlockSpec((tm, C), lambda i: (i, 0)),
            pl.BlockSpec((POOL_HALO, C), lambda i: (jnp.maximum(i * halo_blocks - 1, 0), 0)),
            pl.BlockSpec((POOL_HALO, C), lambda i: (jnp.minimum((i + 1) * halo_blocks, last_halo), 0)),
            pl.BlockSpec((tm, D), lambda i: (i, 0)),
            pl.BlockSpec((tm, D), lambda i: (i, 1)),
            pl.BlockSpec((tm, D), lambda i: (i, 0)),
            _resident(w_pool.shape), _resident(pool_scale.shape),
            _resident(w_ba.shape), _resident(w_bp.shape), _resident(w_out.shape),
        ],
        out_specs=pl.BlockSpec((tm, D), lambda i: (i, 0)),
        scratch_shapes=[pltpu.VMEM((tm + 2 * POOL_HALO, C), F32), pltpu.VMEM((tm, C), BF16)],
        compiler_params=_params(("parallel",)),
        name="pool_merge_out",
    )(attn, z, z, z, gates, gates, h, w_pool, pool_scale, w_ba, w_bp, w_out)


def _trunk(x, w, *, seq_len):
    B, S, D = x.shape
    x2 = x.reshape(B * S, D)
    h1, u = _ffn(x2, w["g_ffn1"], w["w1_gate"], w["w1_up"], w["w1_down"], w["g_mix"], final=False)
    q, k, v, z, gates = _inproj(u, w["w_in"])
    attn = _attention(q, k, v, w["attn_bias"], img_rows=seq_len // GRID_W)
    h2 = _merge(attn, z, gates, h1, w["w_pool"], w["pool_scale"], w["w_branch_attn"], w["w_branch_pool"],
                w["w_out"], seq_len=seq_len)
    y = _ffn(h2, w["g_ffn2"], w["w2_gate"], w["w2_up"], w["w2_down"], w["g_final"], final=True)
    return y.reshape(B, S, D)


def kernel(x_prompt, x_sample, g_ffn1, w1_gate, w1_up, w1_down, g_mix, w_in, rpb, w_pool, pool_scale,
           w_branch_attn, w_branch_pool, w_out, g_ffn2, w2_gate, w2_up, w2_down, g_final):
    assert g_ffn1.shape[0] == 1, "single-layer block"
    bf = lambda a: a[0].astype(BF16)
    vec = lambda a: a.reshape(1, -1).astype(F32)
    w = {
        "g_ffn1": vec(g_ffn1), "w1_gate": bf(w1_gate), "w1_up": bf(w1_up), "w1_down": bf(w1_down),
        "g_mix": vec(g_mix), "w_in": bf(w_in), "attn_bias": _attn_bias_table(rpb[0]),
        "w_pool": bf(w_pool), "pool_scale": vec(pool_scale),
        "w_branch_attn": bf(w_branch_attn), "w_branch_pool": bf(w_branch_pool), "w_out": bf(w_out),
        "g_ffn2": vec(g_ffn2), "w2_gate": bf(w2_gate), "w2_up": bf(w2_up), "w2_down": bf(w2_down),
        "g_final": vec(g_final),
    }
    y_prompt = _trunk(x_prompt, w, seq_len=x_prompt.shape[1])
    y_sample = _trunk(x_sample, w, seq_len=x_sample.shape[1])
    return (y_prompt, y_sample)
```

```python
import functools

import jax
import jax.numpy as jnp
from jax import lax
from jax.experimental import pallas as pl
from jax.experimental.pallas import tpu as pltpu

F32 = jnp.float32
BF16 = jnp.bfloat16

GRID_W = 64
NA_HEADS = 16
NA_HEAD_DIM = 64
D_ATTN = NA_HEADS * NA_HEAD_DIM
WIN_ROWS = 8
WIN_COLS = 16
POOL_WINDOWS = (2, 4, 8, 16)
POOL_HALO = max(POOL_WINDOWS) // 2
RMS_EPS = 1e-6
MASK_BIAS = -1e30

V7X_LANES = 128
V7X_VMEM_LIMIT_BYTES = 56 * 1024 * 1024

FFN_TOKENS = 1024
FFN_ROW_GROUP = 512
FFN_HIDDEN = 512
PROJ_TOKENS = 1024
PROJ_COLS = 1024
ATTN_ROWS = 8
ATTN_KV_BLOCK_ROWS = 4
ATTN_KV_BLOCKS = 4
ATTN_WIN_ROWS = ATTN_KV_BLOCK_ROWS * ATTN_KV_BLOCKS
MERGE_TOKENS = 512
MERGE_COLS = 512
HEAD_PAIRS = NA_HEADS // 2


def _params(semantics):
    return pltpu.CompilerParams(dimension_semantics=semantics, vmem_limit_bytes=V7X_VMEM_LIMIT_BYTES)


def _resident(shape):
    zeros = (0,) * len(shape)
    return pl.BlockSpec(shape, lambda *_: zeros, pipeline_mode=pl.Buffered(1))


def _rmsnorm(x, g):
    return x * lax.rsqrt(jnp.mean(x * x, axis=-1, keepdims=True) + RMS_EPS) * g


def _sigmoid(x):
    return 1.0 / (1.0 + jnp.exp(-x))


def _ffn_body(x_hbm, gpre_ref, wg_ref, wu_ref, wd_ref, gpost_ref, *rest, final):
    if final:
        acc_ref, xbuf, sem, xn_ref = rest
    else:
        acc_ref, u_ref, xbuf, sem = rest
        xn_ref = u_ref
    i, f = pl.program_id(0), pl.program_id(1)
    n_tiles, n_chunks = pl.num_programs(0), pl.num_programs(1)
    tm = xbuf.shape[0]

    def x_copy(tile):
        return pltpu.make_async_copy(x_hbm.at[pl.ds(tile * tm, tm)], xbuf, sem)

    def rows_step(rows, first, last):
        if first:
            xn = _rmsnorm(xbuf[rows, :], gpre_ref[...]).astype(BF16)
            xn_ref[rows, :] = xn
        else:
            xn = xn_ref[rows, :]
        gate = jnp.dot(xn, wg_ref[...], preferred_element_type=F32)
        up = jnp.dot(xn, wu_ref[...], preferred_element_type=F32)
        hid = (0.5 * gate * _sigmoid(gate) * up).astype(BF16)
        acc = (xbuf[rows, :] if first else acc_ref[rows, :]) + jnp.dot(hid, wd_ref[...], preferred_element_type=F32)
        if not last:
            acc_ref[rows, :] = acc
        elif final:
            acc_ref[rows, :] = _rmsnorm(acc, gpost_ref[...])
        else:
            acc_ref[rows, :] = acc
            u_ref[rows, :] = _rmsnorm(acc, gpost_ref[...]).astype(BF16)

    def tile_step(first, last):
        for r in range(tm // FFN_ROW_GROUP):
            rows_step(pl.ds(r * FFN_ROW_GROUP, FFN_ROW_GROUP), first, last)

    @pl.when(jnp.logical_and(i == 0, f == 0))
    def _():
        x_copy(0).start()

    @pl.when(f == 0)
    def _():
        x_copy(i).wait()
        tile_step(True, False)

    @pl.when(jnp.logical_and(f == 1, i + 1 < n_tiles))
    def _():
        x_copy(i + 1).start()

    @pl.when(jnp.logical_and(f > 0, f < n_chunks - 1))
    def _():
        tile_step(False, False)

    @pl.when(f == n_chunks - 1)
    def _():
        tile_step(False, True)


def _ffn(x, g_pre, wg, wu, wd, g_post, *, final):
    T, D = x.shape
    Fh = wg.shape[1]
    tm, tf = FFN_TOKENS, FFN_HIDDEN
    assert Fh // tf >= 2 and T % tm == 0 and Fh % tf == 0
    row = pl.BlockSpec((tm, D), lambda i, f: (i, 0))
    vec = pl.BlockSpec((1, D), lambda i, f: (0, 0))
    scratch = [pltpu.VMEM((tm, D), F32), pltpu.SemaphoreType.DMA(())]
    if final:
        out_shape = jax.ShapeDtypeStruct((T, D), F32)
        out_specs = row
        scratch.append(pltpu.VMEM((tm, D), BF16))
    else:
        out_shape = (jax.ShapeDtypeStruct((T, D), F32), jax.ShapeDtypeStruct((T, D), BF16))
        out_specs = (row, row)
    return pl.pallas_call(
        functools.partial(_ffn_body, final=final),
        out_shape=out_shape,
        grid=(T // tm, Fh // tf),
        in_specs=[
            pl.BlockSpec(memory_space=pl.ANY), vec,
            pl.BlockSpec((D, tf), lambda i, f: (0, f)),
            pl.BlockSpec((D, tf), lambda i, f: (0, f)),
            pl.BlockSpec((tf, D), lambda i, f: (f, 0)),
            vec,
        ],
        out_specs=out_specs,
        scratch_shapes=scratch,
        compiler_params=_params(("arbitrary", "arbitrary")),
        name="ffn_final" if final else "ffn_mid",
    )(x, g_pre, wg, wu, wd, g_post)


def _inproj_body(u_ref, w_ref, q_ref, k_ref, v_ref, z_ref, gate_ref):
    n = pl.program_id(1)

    def proj():
        return jnp.dot(u_ref[...], w_ref[...], preferred_element_type=F32)

    @pl.when(n == 0)
    def _():
        q_ref[...] = (proj() * (NA_HEAD_DIM ** -0.5)).astype(BF16)

    @pl.when(n == 1)
    def _():
        k_ref[...] = proj().astype(BF16)

    @pl.when(n == 2)
    def _():
        v_ref[...] = proj().astype(BF16)

    @pl.when(n == 3)
    def _():
        z_ref[...] = proj()

    @pl.when(n >= 4)
    def _():
        gate_ref[...] = _sigmoid(proj()).astype(BF16)


def _inproj(u, w_in):
    T, D = u.shape
    d_in = w_in.shape[1]
    tm, tn = PROJ_TOKENS, PROJ_COLS
    n_head_blocks = 4
    d_gate = d_in - n_head_blocks * tn
    once = pl.BlockSpec((tm, tn), lambda i, n: (i, 0))
    return pl.pallas_call(
        _inproj_body,
        out_shape=(
            jax.ShapeDtypeStruct((T, tn), BF16),
            jax.ShapeDtypeStruct((T, tn), BF16),
            jax.ShapeDtypeStruct((T, tn), BF16),
            jax.ShapeDtypeStruct((T, tn), F32),
            jax.ShapeDtypeStruct((T, d_gate), BF16),
        ),
        grid=(T // tm, d_in // tn),
        in_specs=[
            pl.BlockSpec((tm, D), lambda i, n: (i, 0)),
            pl.BlockSpec((D, tn), lambda i, n: (0, n)),
        ],
        out_specs=(
            once, once, once, once,
            pl.BlockSpec((tm, tn), lambda i, n: (i, jnp.maximum(n - n_head_blocks, 0))),
        ),
        compiler_params=_params(("parallel", "arbitrary")),
        name="in_proj",
    )(u, w_in)


def _attn_bias_table(rpb):
    n_rel_rows, n_rel_cols = 2 * WIN_ROWS - 1, 2 * WIN_COLS - 1
    skew = GRID_W + n_rel_cols - 1
    padded = jnp.pad(rpb.astype(F32), ((0, 0), (0, 0), (0, skew + 1 - n_rel_cols)))
    tiled = jnp.broadcast_to(padded[:, :, None, :], (NA_HEADS, n_rel_rows, GRID_W, skew + 1))
    skewed = tiled.reshape(NA_HEADS, n_rel_rows, GRID_W * (skew + 1))[:, :, :GRID_W * skew]
    toeplitz = skewed.reshape(NA_HEADS, n_rel_rows, GRID_W, skew)[:, :, :, WIN_COLS - 1:WIN_COLS - 1 + GRID_W]
    c = jnp.arange(GRID_W)
    cs = jnp.clip(c - WIN_COLS // 2, 0, GRID_W - WIN_COLS)
    kc = jnp.arange(GRID_W)
    valid = (kc[None, :] >= cs[:, None]) & (kc[None, :] < cs[:, None] + WIN_COLS)
    by_col = jnp.where(valid[None, None], toeplitz, MASK_BIAS).transpose(0, 2, 1, 3)
    bands = [by_col[:, :, o:o + WIN_ROWS].reshape(NA_HEADS, GRID_W, WIN_ROWS * GRID_W) for o in range(WIN_ROWS)]
    return jnp.stack(bands).reshape(WIN_ROWS, HEAD_PAIRS, 2 * GRID_W, WIN_ROWS * GRID_W)


def _attn_window_start(step, img_rows):
    row0 = step * ATTN_ROWS
    img0 = (row0 // img_rows) * img_rows
    r0 = row0 - img0
    ws = jnp.clip(r0 - WIN_ROWS // 2, 0, img_rows - ATTN_WIN_ROWS)
    return img0 + ws, r0, ws


def _attn_body(q_ref, *rest, img_rows):
    k_blocks = rest[:ATTN_KV_BLOCKS]
    v_blocks = rest[ATTN_KV_BLOCKS:2 * ATTN_KV_BLOCKS]
    bias_ref, o_ref, kw_ref, vw_ref = rest[2 * ATTN_KV_BLOCKS:]
    blk = ATTN_KV_BLOCK_ROWS * GRID_W
    for j in range(ATTN_KV_BLOCKS):
        kw_ref[j * blk:(j + 1) * blk, :] = k_blocks[j][...]
        vw_ref[j * blk:(j + 1) * blk, :] = v_blocks[j][...]

    _, r0, ws = _attn_window_start(pl.program_id(0), img_rows)
    lane = lax.broadcasted_iota(jnp.int32, (GRID_W, V7X_LANES), 1)
    first_head = lane < NA_HEAD_DIM
    band = WIN_ROWS * GRID_W

    def scores(i, p, key0, off):
        lanes = slice(p * V7X_LANES, (p + 1) * V7X_LANES)
        qp = q_ref[i * GRID_W:(i + 1) * GRID_W, lanes]
        zero = jnp.zeros_like(qp)
        lhs = jnp.concatenate([jnp.where(first_head, qp, zero), jnp.where(first_head, zero, qp)], axis=0)
        kb = kw_ref[pl.ds(key0, band), lanes]
        s = lax.dot_general(lhs, kb, (((1,), (1,)), ((), ())), preferred_element_type=F32)
        return s + bias_ref[off, p]

    def row_max(s):
        return jnp.max(s, axis=-1, keepdims=True)

    def softmax(s, m):
        e = jnp.exp(s - m)
        return e.astype(BF16), 1.0 / jnp.sum(e, axis=-1, keepdims=True)

    def values(i, p, key0, e, inv):
        lanes = slice(p * V7X_LANES, (p + 1) * V7X_LANES)
        vb = vw_ref[pl.ds(key0, band), lanes]
        o = jnp.dot(e, vb, preferred_element_type=F32) * inv
        o_ref[i * GRID_W:(i + 1) * GRID_W, lanes] = jnp.where(first_head, o[:GRID_W], o[GRID_W:]).astype(BF16)

    items = []
    for i in range(ATTN_ROWS):
        r = r0 + i
        rs = jnp.clip(r - WIN_ROWS // 2, 0, img_rows - WIN_ROWS)
        key0 = pl.multiple_of((rs - ws) * GRID_W, GRID_W)
        off = rs - r + (WIN_ROWS - 1)
        items += [(i, p, key0, off) for p in range(HEAD_PAIRS)]
    scored, maxed, normed = None, None, None
    for n in range(len(items) + 3):
        new_scored = None
        if n < len(items):
            i, p, key0, off = items[n]
            new_scored = (i, p, key0, scores(i, p, key0, off))
        new_maxed = None
        if scored is not None:
            new_maxed = scored + (row_max(scored[3]),)
        new_normed = None
        if maxed is not None:
            i, p, key0, s, m = maxed
            new_normed = (i, p, key0) + softmax(s, m)
        if normed is not None:
            values(*normed)
        scored, maxed, normed = new_scored, new_maxed, new_normed


def _attention(q, k, v, bias, *, img_rows):
    T, C = q.shape
    tq = ATTN_ROWS * GRID_W
    blk = ATTN_KV_BLOCK_ROWS * GRID_W

    def kv_spec(j):
        def index(s):
            win0, _, _ = _attn_window_start(s, img_rows)
            return (win0 // ATTN_KV_BLOCK_ROWS + j, 0)
        return pl.BlockSpec((blk, C), index)

    kv_specs = [kv_spec(j) for j in range(ATTN_KV_BLOCKS)]
    return pl.pallas_call(
        functools.partial(_attn_body, img_rows=img_rows),
        out_shape=jax.ShapeDtypeStruct((T, C), BF16),
        grid=(T // tq,),
        in_specs=[pl.BlockSpec((tq, C), lambda s: (s, 0))] + kv_specs + kv_specs + [_resident(bias.shape)],
        out_specs=pl.BlockSpec((tq, C), lambda s: (s, 0)),
        scratch_shapes=[pltpu.VMEM((ATTN_WIN_ROWS * GRID_W, C), BF16)] * 2,
        compiler_params=_params(("parallel",)),
        name="nbr_attention",
    )(q, *([k] * ATTN_KV_BLOCKS), *([v] * ATTN_KV_BLOCKS), bias)


def _merge_body(attn_ref, z_ref, zprev_ref, znext_ref, ga_ref, gp_ref, h_ref, wpool_ref, scale_ref,
                wba_ref, wbp_ref, wout_ref, o_ref, ext_ref, pooled_ref, mix_ref, merged_ref, *, seq_len):
    tm = z_ref.shape[0]
    gd = wpool_ref.shape[1]
    t0 = (pl.program_id(0) * tm) % seq_len
    ext_ref[0:POOL_HALO, :] = jnp.where(t0 == 0, 0.0, zprev_ref[...])
    ext_ref[POOL_HALO:POOL_HALO + tm, :] = z_ref[...]
    ext_ref[POOL_HALO + tm:, :] = jnp.where(t0 + tm == seq_len, 0.0, znext_ref[...])
    n_out = o_ref.shape[1]
    col_chunks = [slice(c, c + MERGE_COLS) for c in range(0, n_out, MERGE_COLS)]
    tpos = t0 + lax.broadcasted_iota(jnp.int32, (tm, 1), 0)

    def pool_group(g):
        w = POOL_WINDOWS[g]
        cols = slice(g * gd, (g + 1) * gd)
        total = ext_ref[pl.ds(POOL_HALO - w // 2, tm), cols]
        for d in range(-w // 2 + 1, w // 2):
            total = total + ext_ref[pl.ds(POOL_HALO + d, tm), cols]
        cnt = jnp.minimum(tpos + w // 2, seq_len) - jnp.maximum(tpos - w // 2, 0)
        pooled_ref[:, cols] = (total * (1.0 / cnt.astype(F32)) - z_ref[:, cols]).astype(BF16)

    pool_order = sorted(range(len(POOL_WINDOWS)), key=lambda g: -POOL_WINDOWS[g])
    for step in range(max(len(col_chunks), len(pool_order))):
        if step < len(col_chunks):
            cc = col_chunks[step]
            y_attn = jnp.dot(attn_ref[...], wba_ref[:, cc], preferred_element_type=F32)
            o_ref[:, cc] = ga_ref[:, cc].astype(F32) * y_attn
        if step < len(pool_order):
            pool_group(pool_order[step])
    for g in range(len(POOL_WINDOWS)):
        cols = slice(g * gd, (g + 1) * gd)
        mixed = jnp.dot(pooled_ref[:, cols], wpool_ref[g], preferred_element_type=F32)
        mix_ref[:, cols] = (mixed * scale_ref[:, cols]).astype(BF16)
    for cc in col_chunks:
        y_pool = jnp.dot(mix_ref[...], wbp_ref[:, cc], preferred_element_type=F32)
        merged_ref[:, cc] = (o_ref[:, cc] + gp_ref[:, cc].astype(F32) * y_pool).astype(BF16)
    for cc in col_chunks:
        o_ref[:, cc] = h_ref[:, cc] + jnp.dot(merged_ref[...], wout_ref[:, cc], preferred_element_type=F32)


def _merge(attn, z, gates, h, w_pool, pool_scale, w_ba, w_bp, w_out, *, seq_len):
    T, D = h.shape
    C = z.shape[1]
    tm = MERGE_TOKENS
    halo_blocks = tm // POOL_HALO
    last_halo = T // POOL_HALO - 1
    return pl.pallas_call(
        functools.partial(_merge_body, seq_len=seq_len),
        out_shape=jax.ShapeDtypeStruct((T, D), F32),
        grid=(T // tm,),
        in_specs=[
            pl.BlockSpec((tm, attn.shape[1]), lambda i: (i, 0)),
            pl.BlockSpec((tm, C), lambda i: (i, 0)),
            pl.BlockSpec((POOL_HALO, C), lambda i: (jnp.maximum(i * halo_blocks - 1, 0), 0)),
            pl.BlockSpec((POOL_HALO, C), lambda i: (jnp.minimum((i + 1) * halo_blocks, last_halo), 0)),
            pl.BlockSpec((tm, D), lambda i: (i, 0)),
            pl.BlockSpec((tm, D), lambda i: (i, 1)),
            pl.BlockSpec((tm, D), lambda i: (i, 0)),
            _resident(w_pool.shape), _resident(pool_scale.shape),
            _resident(w_ba.shape), _resident(w_bp.shape), _resident(w_out.shape),
        ],
        out_specs=pl.BlockSpec((tm, D), lambda i: (i, 0)),
        scratch_shapes=[pltpu.VMEM((tm + 2 * POOL_HALO, C), F32), pltpu.VMEM((tm, C), BF16),
                        pltpu.VMEM((tm, C), BF16), pltpu.VMEM((tm, D), BF16)],
        compiler_params=_params(("parallel",)),
        name="pool_merge_out",
    )(attn, z, z, z, gates, gates, h, w_pool, pool_scale, w_ba, w_bp, w_out)


def _trunk(x, w, *, seq_len):
    B, S, D = x.shape
    x2 = x.reshape(B * S, D)
    h1, u = _ffn(x2, w["g_ffn1"], w["w1_gate"], w["w1_up"], w["w1_down"], w["g_mix"], final=False)
    q, k, v, z, gates = _inproj(u, w["w_in"])
    attn = _attention(q, k, v, w["attn_bias"], img_rows=seq_len // GRID_W)
    h2 = _merge(attn, z, gates, h1, w["w_pool"], w["pool_scale"], w["w_branch_attn"], w["w_branch_pool"],
                w["w_out"], seq_len=seq_len)
    y = _ffn(h2, w["g_ffn2"], w["w2_gate"], w["w2_up"], w["w2_down"], w["g_final"], final=True)
    return y.reshape(B, S, D)


def kernel(x_prompt, x_sample, g_ffn1, w1_gate, w1_up, w1_down, g_mix, w_in, rpb, w_pool, pool_scale,
           w_branch_attn, w_branch_pool, w_out, g_ffn2, w2_gate, w2_up, w2_down, g_final):
    assert g_ffn1.shape[0] == 1, "single-layer block"
    bf = lambda a: a[0].astype(BF16)
    vec = lambda a: a.reshape(1, -1).astype(F32)
    w = {
        "g_ffn1": vec(g_ffn1), "w1_gate": bf(w1_gate), "w1_up": bf(w1_up), "w1_down": bf(w1_down),
        "g_mix": vec(g_mix), "w_in": bf(w_in), "attn_bias": _attn_bias_table(rpb[0]),
        "w_pool": bf(w_pool), "pool_scale": vec(pool_scale),
        "w_branch_attn": bf(w_branch_attn), "w_branch_pool": bf(w_branch_pool), "w_out": bf(w_out),
        "g_ffn2": vec(g_ffn2), "w2_gate": bf(w2_gate), "w2_up": bf(w2_up), "w2_down": bf(w2_down),
        "g_final": vec(g_final),
    }
    y_prompt = _trunk(x_prompt, w, seq_len=x_prompt.shape[1])
    y_sample = _trunk(x_sample, w, seq_len=x_sample.shape[1])
    return (y_prompt, y_sample)
```

```python
import functools

import jax
import jax.numpy as jnp
from jax import lax
from jax.experimental import pallas as pl
from jax.experimental.pallas import tpu as pltpu

F32 = jnp.float32
BF16 = jnp.bfloat16

GRID_W = 64
NA_HEADS = 16
NA_HEAD_DIM = 64
D_ATTN = NA_HEADS * NA_HEAD_DIM
WIN_ROWS = 8
WIN_COLS = 16
POOL_WINDOWS = (2, 4, 8, 16)
POOL_HALO = max(POOL_WINDOWS) // 2
RMS_EPS = 1e-6
MASK_BIAS = -1e30

V7X_LANES = 128
V7X_VMEM_LIMIT_BYTES = 56 * 1024 * 1024

FFN_TOKENS = 1024
FFN_HIDDEN = 512
PROJ_TOKENS = 1024
PROJ_COLS = 1024
ATTN_ROWS = 8
ATTN_KV_BLOCK_ROWS = 4
ATTN_KV_BLOCKS = 4
ATTN_WIN_ROWS = ATTN_KV_BLOCK_ROWS * ATTN_KV_BLOCKS
MERGE_TOKENS = 512
MERGE_COLS = 512
HEAD_PAIRS = NA_HEADS // 2


def _params(semantics):
    return pltpu.CompilerParams(dimension_semantics=semantics, vmem_limit_bytes=V7X_VMEM_LIMIT_BYTES)


def _resident(shape):
    zeros = (0,) * len(shape)
    return pl.BlockSpec(shape, lambda *_: zeros, pipeline_mode=pl.Buffered(1))


def _rmsnorm(x, g):
    return x * lax.rsqrt(jnp.mean(x * x, axis=-1, keepdims=True) + RMS_EPS) * g


def _sigmoid(x):
    return 1.0 / (1.0 + jnp.exp(-x))


def _ffn_body(x_hbm, gpre_ref, wg_ref, wu_ref, wd_ref, gpost_ref, *rest, final):
    if final:
        acc_ref, xbuf, sem, xn_ref = rest
    else:
        acc_ref, u_ref, xbuf, sem = rest
        xn_ref = u_ref
    i, f = pl.program_id(0), pl.program_id(1)
    n_tiles, n_chunks = pl.num_programs(0), pl.num_programs(1)
    tm = xbuf.shape[0]

    def x_copy(tile):
        return pltpu.make_async_copy(x_hbm.at[pl.ds(tile * tm, tm)], xbuf, sem)

    def tile_step(first, last):
        if first:
            xn = _rmsnorm(xbuf[...], gpre_ref[...]).astype(BF16)
            xn_ref[...] = xn
        else:
            xn = xn_ref[...]
        gate = jnp.dot(xn, wg_ref[...], preferred_element_type=F32)
        up = jnp.dot(xn, wu_ref[...], preferred_element_type=F32)
        hid = (0.5 * gate * _sigmoid(gate) * up).astype(BF16)
        acc = (xbuf[...] if first else acc_ref[...]) + jnp.dot(hid, wd_ref[...], preferred_element_type=F32)
        if not last:
            acc_ref[...] = acc
        elif final:
            acc_ref[...] = _rmsnorm(acc, gpost_ref[...])
        else:
            acc_ref[...] = acc
            u_ref[...] = _rmsnorm(acc, gpost_ref[...]).astype(BF16)

    @pl.when(jnp.logical_and(i == 0, f == 0))
    def _():
        x_copy(0).start()

    @pl.when(f == 0)
    def _():
        x_copy(i).wait()
        tile_step(True, False)

    @pl.when(jnp.logical_and(f == 1, i + 1 < n_tiles))
    def _():
        x_copy(i + 1).start()

    @pl.when(jnp.logical_and(f > 0, f < n_chunks - 1))
    def _():
        tile_step(False, False)

    @pl.when(f == n_chunks - 1)
    def _():
        tile_step(False, True)


def _ffn(x, g_pre, wg, wu, wd, g_post, *, final):
    T, D = x.shape
    Fh = wg.shape[1]
    tm, tf = FFN_TOKENS, FFN_HIDDEN
    assert Fh // tf >= 2 and T % tm == 0 and Fh % tf == 0
    row = pl.BlockSpec((tm, D), lambda i, f: (i, 0))
    vec = pl.BlockSpec((1, D), lambda i, f: (0, 0))
    scratch = [pltpu.VMEM((tm, D), F32), pltpu.SemaphoreType.DMA(())]
    if final:
        out_shape = jax.ShapeDtypeStruct((T, D), F32)
        out_specs = row
        scratch.append(pltpu.VMEM((tm, D), BF16))
    else:
        out_shape = (jax.ShapeDtypeStruct((T, D), F32), jax.ShapeDtypeStruct((T, D), BF16))
        out_specs = (row, row)
    return pl.pallas_call(
        functools.partial(_ffn_body, final=final),
        out_shape=out_shape,
        grid=(T // tm, Fh // tf),
        in_specs=[
            pl.BlockSpec(memory_space=pl.ANY), vec,
            pl.BlockSpec((D, tf), lambda i, f: (0, f)),
            pl.BlockSpec((D, tf), lambda i, f: (0, f)),
            pl.BlockSpec((tf, D), lambda i, f: (f, 0)),
            vec,
        ],
        out_specs=out_specs,
        scratch_shapes=scratch,
        compiler_params=_params(("arbitrary", "arbitrary")),
        name="ffn_final" if final else "ffn_mid",
    )(x, g_pre, wg, wu, wd, g_post)


def _proj_heads_body(u_ref, w_ref, q_ref, k_ref, v_ref, z_ref):
    u = u_ref[...]
    c = q_ref.shape[1]

    def proj(n):
        return jnp.dot(u, w_ref[:, n * c:(n + 1) * c], preferred_element_type=F32)

    q_ref[...] = (proj(0) * (NA_HEAD_DIM ** -0.5)).astype(BF16)
    k_ref[...] = proj(1).astype(BF16)
    v_ref[...] = proj(2).astype(BF16)
    z_ref[...] = proj(3)


def _proj_gates_body(u_ref, w_ref, gate_ref):
    u = u_ref[...]
    for c in range(0, gate_ref.shape[1], PROJ_COLS):
        cols = slice(c, c + PROJ_COLS)
        gate_ref[:, cols] = _sigmoid(jnp.dot(u, w_ref[:, cols], preferred_element_type=F32)).astype(BF16)


def _inproj(u, w_in):
    T, D = u.shape
    tm, c = PROJ_TOKENS, D_ATTN
    d_heads = 4 * c
    d_gate = w_in.shape[1] - d_heads
    assert d_gate == d_heads, "w_in halves are addressed as two equal column blocks"
    u_spec = pl.BlockSpec((tm, D), lambda i: (i, 0))
    head = pl.BlockSpec((tm, c), lambda i: (i, 0))
    q, k, v, z = pl.pallas_call(
        _proj_heads_body,
        out_shape=(
            jax.ShapeDtypeStruct((T, c), BF16),
            jax.ShapeDtypeStruct((T, c), BF16),
            jax.ShapeDtypeStruct((T, c), BF16),
            jax.ShapeDtypeStruct((T, c), F32),
        ),
        grid=(T // tm,),
        in_specs=[u_spec, pl.BlockSpec((D, d_heads), lambda i: (0, 0), pipeline_mode=pl.Buffered(1))],
        out_specs=(head, head, head, head),
        compiler_params=_params(("parallel",)),
        name="proj_heads",
    )(u, w_in)
    gates = pl.pallas_call(
        _proj_gates_body,
        out_shape=jax.ShapeDtypeStruct((T, d_gate), BF16),
        grid=(T // tm,),
        in_specs=[u_spec, pl.BlockSpec((D, d_gate), lambda i: (0, 1), pipeline_mode=pl.Buffered(1))],
        out_specs=pl.BlockSpec((tm, d_gate), lambda i: (i, 0)),
        compiler_params=_params(("parallel",)),
        name="proj_gates",
    )(u, w_in)
    return q, k, v, z, gates


def _attn_bias_table(rpb):
    n_rel_rows, n_rel_cols = 2 * WIN_ROWS - 1, 2 * WIN_COLS - 1
    skew = GRID_W + n_rel_cols - 1
    padded = jnp.pad(rpb.astype(F32), ((0, 0), (0, 0), (0, skew + 1 - n_rel_cols)))
    tiled = jnp.broadcast_to(padded[:, :, None, :], (NA_HEADS, n_rel_rows, GRID_W, skew + 1))
    skewed = tiled.reshape(NA_HEADS, n_rel_rows, GRID_W * (skew + 1))[:, :, :GRID_W * skew]
    toeplitz = skewed.reshape(NA_HEADS, n_rel_rows, GRID_W, skew)[:, :, :, WIN_COLS - 1:WIN_COLS - 1 + GRID_W]
    c = jnp.arange(GRID_W)
    cs = jnp.clip(c - WIN_COLS // 2, 0, GRID_W - WIN_COLS)
    kc = jnp.arange(GRID_W)
    valid = (kc[None, :] >= cs[:, None]) & (kc[None, :] < cs[:, None] + WIN_COLS)
    by_col = jnp.where(valid[None, None], toeplitz, MASK_BIAS).transpose(0, 2, 1, 3)
    bands = [by_col[:, :, o:o + WIN_ROWS].reshape(NA_HEADS, GRID_W, WIN_ROWS * GRID_W) for o in range(WIN_ROWS)]
    return jnp.stack(bands).reshape(WIN_ROWS, HEAD_PAIRS, 2 * GRID_W, WIN_ROWS * GRID_W)


def _attn_window_start(step, img_rows):
    row0 = step * ATTN_ROWS
    img0 = (row0 // img_rows) * img_rows
    r0 = row0 - img0
    ws = jnp.clip(r0 - WIN_ROWS // 2, 0, img_rows - ATTN_WIN_ROWS)
    return img0 + ws, r0, ws


def _attn_body(q_ref, *rest, img_rows):
    k_blocks = rest[:ATTN_KV_BLOCKS]
    v_blocks = rest[ATTN_KV_BLOCKS:2 * ATTN_KV_BLOCKS]
    bias_ref, o_ref, kw_ref, vw_ref = rest[2 * ATTN_KV_BLOCKS:]
    blk = ATTN_KV_BLOCK_ROWS * GRID_W
    for j in range(ATTN_KV_BLOCKS):
        kw_ref[j * blk:(j + 1) * blk, :] = k_blocks[j][...]
        vw_ref[j * blk:(j + 1) * blk, :] = v_blocks[j][...]

    _, r0, ws = _attn_window_start(pl.program_id(0), img_rows)
    lane = lax.broadcasted_iota(jnp.int32, (GRID_W, V7X_LANES), 1)
    first_head = lane < NA_HEAD_DIM
    band = WIN_ROWS * GRID_W

    def scores(i, p, key0, off):
        lanes = slice(p * V7X_LANES, (p + 1) * V7X_LANES)
        qp = q_ref[i * GRID_W:(i + 1) * GRID_W, lanes]
        zero = jnp.zeros_like(qp)
        lhs = jnp.concatenate([jnp.where(first_head, qp, zero), jnp.where(first_head, zero, qp)], axis=0)
        kb = kw_ref[pl.ds(key0, band), lanes]
        s = lax.dot_general(lhs, kb, (((1,), (1,)), ((), ())), preferred_element_type=F32)
        return s + bias_ref[off, p]

    def row_max(s):
        return jnp.max(s, axis=-1, keepdims=True)

    def softmax(s, m):
        e = jnp.exp(s - m)
        return e.astype(BF16), 1.0 / jnp.sum(e, axis=-1, keepdims=True)

    def values(even, odd):
        i, p, key0, _, _ = even
        vb = vw_ref[pl.ds(key0, band), p * V7X_LANES:(p + 2) * V7X_LANES]
        o = jnp.dot(jnp.concatenate([even[3], odd[3]], axis=0), vb, preferred_element_type=F32)
        for a, (_, pa, _, _, inv) in enumerate((even, odd)):
            oa = o[a * 2 * GRID_W:(a + 1) * 2 * GRID_W, a * V7X_LANES:(a + 1) * V7X_LANES] * inv
            o_ref[i * GRID_W:(i + 1) * GRID_W, pa * V7X_LANES:(pa + 1) * V7X_LANES] = (
                jnp.where(first_head, oa[:GRID_W], oa[GRID_W:]).astype(BF16))

    items = []
    for i in range(ATTN_ROWS):
        r = r0 + i
        rs = jnp.clip(r - WIN_ROWS // 2, 0, img_rows - WIN_ROWS)
        key0 = pl.multiple_of((rs - ws) * GRID_W, GRID_W)
        off = rs - r + (WIN_ROWS - 1)
        items += [(i, p, key0, off) for p in range(HEAD_PAIRS)]
    scored, maxed, normed, held = None, None, None, None
    for n in range(len(items) + 3):
        new_scored = None
        if n < len(items):
            i, p, key0, off = items[n]
            new_scored = (i, p, key0, scores(i, p, key0, off))
        new_maxed = None
        if scored is not None:
            new_maxed = scored + (row_max(scored[3]),)
        new_normed = None
        if maxed is not None:
            i, p, key0, s, m = maxed
            new_normed = (i, p, key0) + softmax(s, m)
        if normed is not None:
            if normed[1] % 2 == 0:
                held = normed
            else:
                values(held, normed)
        scored, maxed, normed = new_scored, new_maxed, new_normed


def _attention(q, k, v, bias, *, img_rows):
    T, C = q.shape
    tq = ATTN_ROWS * GRID_W
    blk = ATTN_KV_BLOCK_ROWS * GRID_W

    def kv_spec(j):
        def index(s):
            win0, _, _ = _attn_window_start(s, img_rows)
            return (win0 // ATTN_KV_BLOCK_ROWS + j, 0)
        return pl.BlockSpec((blk, C), index)

    kv_specs = [kv_spec(j) for j in range(ATTN_KV_BLOCKS)]
    return pl.pallas_call(
        functools.partial(_attn_body, img_rows=img_rows),
        out_shape=jax.ShapeDtypeStruct((T, C), BF16),
        grid=(T // tq,),
        in_specs=[pl.BlockSpec((tq, C), lambda s: (s, 0))] + kv_specs + kv_specs + [_resident(bias.shape)],
        out_specs=pl.BlockSpec((tq, C), lambda s: (s, 0)),
        scratch_shapes=[pltpu.VMEM((ATTN_WIN_ROWS * GRID_W, C), BF16)] * 2,
        compiler_params=_params(("parallel",)),
        name="nbr_attention",
    )(q, *([k] * ATTN_KV_BLOCKS), *([v] * ATTN_KV_BLOCKS), bias)


def _merge_body(attn_ref, z_ref, zprev_ref, znext_ref, ga_ref, gp_ref, h_ref, wpool_ref, scale_ref,
                wba_ref, wbp_ref, wout_ref, o_ref, ext_ref, pooled_ref, mix_ref, merged_ref, *, seq_len):
    tm = z_ref.shape[0]
    gd = wpool_ref.shape[1]
    t0 = (pl.program_id(0) * tm) % seq_len
    ext_ref[0:POOL_HALO, :] = jnp.where(t0 == 0, 0.0, zprev_ref[...])
    ext_ref[POOL_HALO:POOL_HALO + tm, :] = z_ref[...]
    ext_ref[POOL_HALO + tm:, :] = jnp.where(t0 + tm == seq_len, 0.0, znext_ref[...])
    n_out = o_ref.shape[1]
    col_chunks = [slice(c, c + MERGE_COLS) for c in range(0, n_out, MERGE_COLS)]
    tpos = t0 + lax.broadcasted_iota(jnp.int32, (tm, 1), 0)

    def pool_group(g):
        w = POOL_WINDOWS[g]
        cols = slice(g * gd, (g + 1) * gd)
        total = ext_ref[pl.ds(POOL_HALO - w // 2, tm), cols]
        for d in range(-w // 2 + 1, w // 2):
            total = total + ext_ref[pl.ds(POOL_HALO + d, tm), cols]
        cnt = jnp.minimum(tpos + w // 2, seq_len) - jnp.maximum(tpos - w // 2, 0)
        pooled_ref[:, cols] = (total * (1.0 / cnt.astype(F32)) - z_ref[:, cols]).astype(BF16)

    pool_order = sorted(range(len(POOL_WINDOWS)), key=lambda g: -POOL_WINDOWS[g])
    for step in range(max(len(col_chunks), len(pool_order))):
        if step < len(col_chunks):
            cc = col_chunks[step]
            y_attn = jnp.dot(attn_ref[...], wba_ref[:, cc], preferred_element_type=F32)
            o_ref[:, cc] = ga_ref[:, cc].astype(F32) * y_attn
        if step < len(pool_order):
            pool_group(pool_order[step])
    for g in range(len(POOL_WINDOWS)):
        cols = slice(g * gd, (g + 1) * gd)
        mixed = jnp.dot(pooled_ref[:, cols], wpool_ref[g], preferred_element_type=F32)
        mix_ref[:, cols] = (mixed * scale_ref[:, cols]).astype(BF16)
    for cc in col_chunks:
        y_pool = jnp.dot(mix_ref[...], wbp_ref[:, cc], preferred_element_type=F32)
        merged_ref[:, cc] = (o_ref[:, cc] + gp_ref[:, cc].astype(F32) * y_pool).astype(BF16)
    for cc in col_chunks:
        o_ref[:, cc] = h_ref[:, cc] + jnp.dot(merged_ref[...], wout_ref[:, cc], preferred_element_type=F32)


def _merge(attn, z, gates, h, w_pool, pool_scale, w_ba, w_bp, w_out, *, seq_len):
    T, D = h.shape
    C = z.shape[1]
    tm = MERGE_TOKENS
    halo_blocks = tm // POOL_HALO
    last_halo = T // POOL_HALO - 1
    return pl.pallas_call(
        functools.partial(_merge_body, seq_len=seq_len),
        out_shape=jax.ShapeDtypeStruct((T, D), F32),
        grid=(T // tm,),
        in_specs=[
            pl.BlockSpec((tm, attn.shape[1]), lambda i: (i, 0)),
            pl.BlockSpec((tm, C), lambda i: (i, 0)),
            pl.BlockSpec((POOL_HALO, C), lambda i: (jnp.maximum(i * halo_blocks - 1, 0), 0)),
            pl.BlockSpec((POOL_HALO, C), lambda i: (jnp.minimum((i + 1) * halo_blocks, last_halo), 0)),
            pl.BlockSpec((tm, D), lambda i: (i, 0)),
            pl.BlockSpec((tm, D), lambda i: (i, 1)),
            pl.BlockSpec((tm, D), lambda i: (i, 0)),
            _resident(w_pool.shape), _resident(pool_scale.shape),
            _resident(w_ba.shape), _resident(w_bp.shape), _resident(w_out.shape),
        ],
        out_specs=pl.BlockSpec((tm, D), lambda i: (i, 0)),
        scratch_shapes=[pltpu.VMEM((tm + 2 * POOL_HALO, C), F32), pltpu.VMEM((tm, C), BF16),
                        pltpu.VMEM((tm, C), BF16), pltpu.VMEM((tm, D), BF16)],
        compiler_params=_params(("parallel",)),
        name="pool_merge_out",
    )(attn, z, z, z, gates, gates, h, w_pool, pool_scale, w_ba, w_bp, w_out)


def _trunk(x, w, *, seq_len):
    B, S, D = x.shape
    x2 = x.reshape(B * S, D)
    h1, u = _ffn(x2, w["g_ffn1"], w["w1_gate"], w["w1_up"], w["w1_down"], w["g_mix"], final=False)
    q, k, v, z, gates = _inproj(u, w["w_in"])
    attn = _attention(q, k, v, w["attn_bias"], img_rows=seq_len // GRID_W)
    h2 = _merge(attn, z, gates, h1, w["w_pool"], w["pool_scale"], w["w_branch_attn"], w["w_branch_pool"],
                w["w_out"], seq_len=seq_len)
    y = _ffn(h2, w["g_ffn2"], w["w2_gate"], w["w2_up"], w["w2_down"], w["g_final"], final=True)
    return y.reshape(B, S, D)


def kernel(x_prompt, x_sample, g_ffn1, w1_gate, w1_up, w1_down, g_mix, w_in, rpb, w_pool, pool_scale,
           w_branch_attn, w_branch_pool, w_out, g_ffn2, w2_gate, w2_up, w2_down, g_final):
    assert g_ffn1.shape[0] == 1, "single-layer block"
    bf = lambda a: a[0].astype(BF16)
    vec = lambda a: a.reshape(1, -1).astype(F32)
    w = {
        "g_ffn1": vec(g_ffn1), "w1_gate": bf(w1_gate), "w1_up": bf(w1_up), "w1_down": bf(w1_down),
        "g_mix": vec(g_mix), "w_in": bf(w_in), "attn_bias": _attn_bias_table(rpb[0]),
        "w_pool": bf(w_pool), "pool_scale": vec(pool_scale),
        "w_branch_attn": bf(w_branch_attn), "w_branch_pool": bf(w_branch_pool), "w_out": bf(w_out),
        "g_ffn2": vec(g_ffn2), "w2_gate": bf(w2_gate), "w2_up": bf(w2_up), "w2_down": bf(w2_down),
        "g_final": vec(g_final),
    }
    y_prompt = _trunk(x_prompt, w, seq_len=x_prompt.shape[1])
    y_sample = _trunk(x_sample, w, seq_len=x_sample.shape[1])
    return (y_prompt, y_sample)
```

```python
import functools

import jax
import jax.numpy as jnp
from jax import lax
from jax.experimental import pallas as pl
from jax.experimental.pallas import tpu as pltpu

F32 = jnp.float32
BF16 = jnp.bfloat16

GRID_W = 64
NA_HEADS = 16
NA_HEAD_DIM = 64
D_ATTN = NA_HEADS * NA_HEAD_DIM
WIN_ROWS = 8
WIN_COLS = 16
POOL_WINDOWS = (2, 4, 8, 16)
POOL_HALO = max(POOL_WINDOWS) // 2
RMS_EPS = 1e-6
MASK_BIAS = -1e30

V7X_LANES = 128
V7X_VMEM_LIMIT_BYTES = 56 * 1024 * 1024

FFN_TOKENS = 1024
FFN_HIDDEN = 512
PROJ_TOKENS = 1024
PROJ_COLS = 1024
ATTN_ROWS = 8
ATTN_KV_BLOCK_ROWS = 4
ATTN_KV_BLOCKS = 4
ATTN_WIN_ROWS = ATTN_KV_BLOCK_ROWS * ATTN_KV_BLOCKS
MERGE_TOKENS = 512
MERGE_COLS = 512
HEAD_PAIRS = NA_HEADS // 2


def _params(semantics):
    return pltpu.CompilerParams(dimension_semantics=semantics, vmem_limit_bytes=V7X_VMEM_LIMIT_BYTES)


def _resident(shape):
    zeros = (0,) * len(shape)
    return pl.BlockSpec(shape, lambda *_: zeros, pipeline_mode=pl.Buffered(1))


def _rmsnorm(x, g):
    return x * lax.rsqrt(jnp.mean(x * x, axis=-1, keepdims=True) + RMS_EPS) * g


def _sigmoid(x):
    return 1.0 / (1.0 + jnp.exp(-x))


def _ffn_body(x_hbm, gpre_ref, wg_ref, wu_ref, wd_ref, gpost_ref, *rest, final):
    if final:
        acc_ref, xbuf, sem, xn_ref = rest
    else:
        acc_ref, u_ref, xbuf, sem = rest
        xn_ref = u_ref
    i, f = pl.program_id(0), pl.program_id(1)
    n_tiles, n_chunks = pl.num_programs(0), pl.num_programs(1)
    tm = xbuf.shape[0]

    def x_copy(tile):
        return pltpu.make_async_copy(x_hbm.at[pl.ds(tile * tm, tm)], xbuf, sem)

    def tile_step(first, last):
        if first:
            xn = _rmsnorm(xbuf[...], gpre_ref[...]).astype(BF16)
            xn_ref[...] = xn
        else:
            xn = xn_ref[...]
        gate = jnp.dot(xn, wg_ref[...], preferred_element_type=F32)
        up = jnp.dot(xn, wu_ref[...], preferred_element_type=F32)
        hid = (0.5 * gate * _sigmoid(gate) * up).astype(BF16)
        acc = (xbuf[...] if first else acc_ref[...]) + jnp.dot(hid, wd_ref[...], preferred_element_type=F32)
        if not last:
            acc_ref[...] = acc
        elif final:
            acc_ref[...] = _rmsnorm(acc, gpost_ref[...])
        else:
            acc_ref[...] = acc
            u_ref[...] = _rmsnorm(acc, gpost_ref[...]).astype(BF16)

    @pl.when(jnp.logical_and(i == 0, f == 0))
    def _():
        x_copy(0).start()

    @pl.when(f == 0)
    def _():
        x_copy(i).wait()
        tile_step(True, False)

    @pl.when(jnp.logical_and(f == 1, i + 1 < n_tiles))
    def _():
        x_copy(i + 1).start()

    @pl.when(jnp.logical_and(f > 0, f < n_chunks - 1))
    def _():
        tile_step(False, False)

    @pl.when(f == n_chunks - 1)
    def _():
        tile_step(False, True)


def _ffn(x, g_pre, wg, wu, wd, g_post, *, final):
    T, D = x.shape
    Fh = wg.shape[1]
    tm, tf = FFN_TOKENS, FFN_HIDDEN
    assert Fh // tf >= 2 and T % tm == 0 and Fh % tf == 0
    row = pl.BlockSpec((tm, D), lambda i, f: (i, 0))
    vec = pl.BlockSpec((1, D), lambda i, f: (0, 0))
    scratch = [pltpu.VMEM((tm, D), F32), pltpu.SemaphoreType.DMA(())]
    if final:
        out_shape = jax.ShapeDtypeStruct((T, D), F32)
        out_specs = row
        scratch.append(pltpu.VMEM((tm, D), BF16))
    else:
        out_shape = (jax.ShapeDtypeStruct((T, D), F32), jax.ShapeDtypeStruct((T, D), BF16))
        out_specs = (row, row)
    return pl.pallas_call(
        functools.partial(_ffn_body, final=final),
        out_shape=out_shape,
        grid=(T // tm, Fh // tf),
        in_specs=[
            pl.BlockSpec(memory_space=pl.ANY), vec,
            pl.BlockSpec((D, tf), lambda i, f: (0, f)),
            pl.BlockSpec((D, tf), lambda i, f: (0, f)),
            pl.BlockSpec((tf, D), lambda i, f: (f, 0)),
            vec,
        ],
        out_specs=out_specs,
        scratch_shapes=scratch,
        compiler_params=_params(("arbitrary", "arbitrary")),
        name="ffn_final" if final else "ffn_mid",
    )(x, g_pre, wg, wu, wd, g_post)


def _proj_heads_body(u_ref, w_ref, q_ref, k_ref, v_ref, z_ref):
    u = u_ref[...]
    c = q_ref.shape[1]

    def proj(n):
        return jnp.dot(u, w_ref[:, n * c:(n + 1) * c], preferred_element_type=F32)

    q_ref[...] = (proj(0) * (NA_HEAD_DIM ** -0.5)).astype(BF16)
    k_ref[...] = proj(1).astype(BF16)
    v_ref[...] = proj(2).astype(BF16)
    z_ref[...] = proj(3)


def _proj_gates_body(u_ref, w_ref, gate_ref):
    u = u_ref[...]
    for c in range(0, gate_ref.shape[1], PROJ_COLS):
        cols = slice(c, c + PROJ_COLS)
        logits = jnp.dot(u, w_ref[:, cols], preferred_element_type=F32)
        gate_ref[:, cols] = (0.5 * jnp.tanh(0.5 * logits) + 0.5).astype(BF16)


def _inproj(u, w_in):
    T, D = u.shape
    tm, c = PROJ_TOKENS, D_ATTN
    d_heads = 4 * c
    d_gate = w_in.shape[1] - d_heads
    assert d_gate == d_heads, "w_in halves are addressed as two equal column blocks"
    u_spec = pl.BlockSpec((tm, D), lambda i: (i, 0))
    head = pl.BlockSpec((tm, c), lambda i: (i, 0))
    q, k, v, z = pl.pallas_call(
        _proj_heads_body,
        out_shape=(
            jax.ShapeDtypeStruct((T, c), BF16),
            jax.ShapeDtypeStruct((T, c), BF16),
            jax.ShapeDtypeStruct((T, c), BF16),
            jax.ShapeDtypeStruct((T, c), F32),
        ),
        grid=(T // tm,),
        in_specs=[u_spec, pl.BlockSpec((D, d_heads), lambda i: (0, 0), pipeline_mode=pl.Buffered(1))],
        out_specs=(head, head, head, head),
        compiler_params=_params(("parallel",)),
        name="proj_heads",
    )(u, w_in)
    gates = pl.pallas_call(
        _proj_gates_body,
        out_shape=jax.ShapeDtypeStruct((T, d_gate), BF16),
        grid=(T // tm,),
        in_specs=[u_spec, pl.BlockSpec((D, d_gate), lambda i: (0, 1), pipeline_mode=pl.Buffered(1))],
        out_specs=pl.BlockSpec((tm, d_gate), lambda i: (i, 0)),
        compiler_params=_params(("parallel",)),
        name="proj_gates",
    )(u, w_in)
    return q, k, v, z, gates


N_REL_ROWS = 2 * WIN_ROWS - 1
N_REL_COLS = 2 * WIN_COLS - 1


def _bias_table_body(rpb_ref, out_ref):
    c = lax.broadcasted_iota(jnp.int32, (GRID_W, V7X_LANES), 0)
    lane = lax.broadcasted_iota(jnp.int32, (GRID_W, V7X_LANES), 1)
    kc = lane % GRID_W
    cs = jnp.clip(c - WIN_COLS // 2, 0, GRID_W - WIN_COLS)
    valid = (kc >= cs) & (kc < cs + WIN_COLS)
    low = lane < GRID_W
    for a in range(2):
        even, odd = [], []
        for dr in range(N_REL_ROWS):
            x = jnp.broadcast_to(rpb_ref[a, dr:dr + 1, :], (GRID_W, V7X_LANES))
            shift = V7X_LANES - (WIN_COLS - 1)
            even.append(jnp.where(valid, pltpu.roll(x, shift, 1, stride=1, stride_axis=0), MASK_BIAS))
            odd.append(jnp.where(valid, pltpu.roll(x, shift + GRID_W, 1, stride=1, stride_axis=0), MASK_BIAS))
        for off in range(WIN_ROWS):
            for jp in range(WIN_ROWS // 2):
                tile = jnp.where(low, even[off + 2 * jp], odd[off + 2 * jp + 1])
                out_ref[off, 0, a * GRID_W:(a + 1) * GRID_W, jp * V7X_LANES:(jp + 1) * V7X_LANES] = tile


def _attn_bias_table(rpb):
    rpb_lanes = jnp.pad(rpb.astype(F32), ((0, 0), (0, 0), (0, V7X_LANES - N_REL_COLS)))
    return pl.pallas_call(
        _bias_table_body,
        out_shape=jax.ShapeDtypeStruct((WIN_ROWS, HEAD_PAIRS, 2 * GRID_W, WIN_ROWS * GRID_W), F32),
        grid=(HEAD_PAIRS,),
        in_specs=[pl.BlockSpec((2, N_REL_ROWS, V7X_LANES), lambda p: (p, 0, 0))],
        out_specs=pl.BlockSpec((WIN_ROWS, 1, 2 * GRID_W, WIN_ROWS * GRID_W), lambda p: (0, p, 0, 0)),
        compiler_params=_params(("parallel",)),
        name="attn_bias_table",
    )(rpb_lanes)


def _attn_window_start(step, img_rows):
    row0 = step * ATTN_ROWS
    img0 = (row0 // img_rows) * img_rows
    r0 = row0 - img0
    ws = jnp.clip(r0 - WIN_ROWS // 2, 0, img_rows - ATTN_WIN_ROWS)
    return img0 + ws, r0, ws


def _attn_body(q_ref, *rest, img_rows):
    k_blocks = rest[:ATTN_KV_BLOCKS]
    v_blocks = rest[ATTN_KV_BLOCKS:2 * ATTN_KV_BLOCKS]
    bias_ref, o_ref, kw_ref, vw_ref = rest[2 * ATTN_KV_BLOCKS:]
    blk = ATTN_KV_BLOCK_ROWS * GRID_W
    for j in range(ATTN_KV_BLOCKS):
        kw_ref[j * blk:(j + 1) * blk, :] = k_blocks[j][...]
        vw_ref[j * blk:(j + 1) * blk, :] = v_blocks[j][...]

    _, r0, ws = _attn_window_start(pl.program_id(0), img_rows)
    lane = lax.broadcasted_iota(jnp.int32, (GRID_W, V7X_LANES), 1)
    first_head = lane < NA_HEAD_DIM
    band = WIN_ROWS * GRID_W

    def scores(i, p, key0, off):
        lanes = slice(p * V7X_LANES, (p + 1) * V7X_LANES)
        qp = q_ref[i * GRID_W:(i + 1) * GRID_W, lanes]
        zero = jnp.zeros_like(qp)
        lhs = jnp.concatenate([jnp.where(first_head, qp, zero), jnp.where(first_head, zero, qp)], axis=0)
        kb = kw_ref[pl.ds(key0, band), lanes]
        s = lax.dot_general(lhs, kb, (((1,), (1,)), ((), ())), preferred_element_type=F32)
        return s + bias_ref[off, p]

    def row_max(s):
        return jnp.max(s, axis=-1, keepdims=True)

    def softmax(s, m):
        e = jnp.exp(s - m)
        return e.astype(BF16), 1.0 / jnp.sum(e, axis=-1, keepdims=True)

    def values(even, odd):
        i, p, key0, _, _ = even
        vb = vw_ref[pl.ds(key0, band), p * V7X_LANES:(p + 2) * V7X_LANES]
        o = jnp.dot(jnp.concatenate([even[3], odd[3]], axis=0), vb, preferred_element_type=F32)
        for a, (_, pa, _, _, inv) in enumerate((even, odd)):
            oa = o[a * 2 * GRID_W:(a + 1) * 2 * GRID_W, a * V7X_LANES:(a + 1) * V7X_LANES] * inv
            o_ref[i * GRID_W:(i + 1) * GRID_W, pa * V7X_LANES:(pa + 1) * V7X_LANES] = (
                jnp.where(first_head, oa[:GRID_W], oa[GRID_W:]).astype(BF16))

    items = []
    for i in range(ATTN_ROWS):
        r = r0 + i
        rs = jnp.clip(r - WIN_ROWS // 2, 0, img_rows - WIN_ROWS)
        key0 = pl.multiple_of((rs - ws) * GRID_W, GRID_W)
        off = rs - r + (WIN_ROWS - 1)
        items += [(i, p, key0, off) for p in range(HEAD_PAIRS)]
    scored, maxed, normed, held = None, None, None, None
    for n in range(len(items) + 3):
        new_scored = None
        if n < len(items):
            i, p, key0, off = items[n]
            new_scored = (i, p, key0, scores(i, p, key0, off))
        new_maxed = None
        if scored is not None:
            new_maxed = scored + (row_max(scored[3]),)
        new_normed = None
        if maxed is not None:
            i, p, key0, s, m = maxed
            new_normed = (i, p, key0) + softmax(s, m)
        if normed is not None:
            if normed[1] % 2 == 0:
                held = normed
            else:
                values(held, normed)
        scored, maxed, normed = new_scored, new_maxed, new_normed


def _attention(q, k, v, bias, *, img_rows):
    T, C = q.shape
    tq = ATTN_ROWS * GRID_W
    blk = ATTN_KV_BLOCK_ROWS * GRID_W

    def kv_spec(j):
        def index(s):
            win0, _, _ = _attn_window_start(s, img_rows)
            return (win0 // ATTN_KV_BLOCK_ROWS + j, 0)
        return pl.BlockSpec((blk, C), index)

    kv_specs = [kv_spec(j) for j in range(ATTN_KV_BLOCKS)]
    return pl.pallas_call(
        functools.partial(_attn_body, img_rows=img_rows),
        out_shape=jax.ShapeDtypeStruct((T, C), BF16),
        grid=(T // tq,),
        in_specs=[pl.BlockSpec((tq, C), lambda s: (s, 0))] + kv_specs + kv_specs + [_resident(bias.shape)],
        out_specs=pl.BlockSpec((tq, C), lambda s: (s, 0)),
        scratch_shapes=[pltpu.VMEM((ATTN_WIN_ROWS * GRID_W, C), BF16)] * 2,
        compiler_params=_params(("parallel",)),
        name="nbr_attention",
    )(q, *([k] * ATTN_KV_BLOCKS), *([v] * ATTN_KV_BLOCKS), bias)


def _window_sum(ext_ref, stage_refs, cols, w, tm):
    rows = tm + 2 * POOL_HALO
    src, src_cols, span, n_stages = ext_ref, cols, 1, 0
    while 2 * span < w:
        dst = stage_refs[n_stages % 2]
        dst[0:rows, :] = src[0:rows, src_cols] + src[span:span + rows, src_cols]
        src, src_cols, span, n_stages = dst, slice(None), 2 * span, n_stages + 1
    lo = POOL_HALO - w // 2
    return src[lo:lo + tm, src_cols] + src[POOL_HALO:POOL_HALO + tm, src_cols]


def _merge_body(attn_ref, z_ref, zprev_ref, znext_ref, ga_ref, gp_ref, h_ref, wpool_ref, scale_ref,
                wba_ref, wbp_ref, wout_ref, o_ref, ext_ref, stage0_ref, stage1_ref, mix_ref, merged_ref, *, seq_len):
    tm = z_ref.shape[0]
    gd = wpool_ref.shape[1]
    t0 = (pl.program_id(0) * tm) % seq_len
    ext_ref[0:POOL_HALO, :] = jnp.where(t0 == 0, 0.0, zprev_ref[...])
    ext_ref[POOL_HALO:POOL_HALO + tm, :] = z_ref[...]
    ext_ref[POOL_HALO + tm:2 * POOL_HALO + tm, :] = jnp.where(t0 + tm == seq_len, 0.0, znext_ref[...])
    for ref in (ext_ref, stage0_ref, stage1_ref):
        ref[2 * POOL_HALO + tm:, :] = jnp.zeros((POOL_HALO, ref.shape[1]), F32)
    n_out = o_ref.shape[1]
    col_chunks = [slice(c, c + MERGE_COLS) for c in range(0, n_out, MERGE_COLS)]
    tpos = t0 + lax.broadcasted_iota(jnp.int32, (tm, 1), 0)

    for cc in col_chunks:
        y_attn = jnp.dot(attn_ref[...], wba_ref[:, cc], preferred_element_type=F32)
        o_ref[:, cc] = ga_ref[:, cc].astype(F32) * y_attn
    for g, w in enumerate(POOL_WINDOWS):
        cols = slice(g * gd, (g + 1) * gd)
        total = _window_sum(ext_ref, (stage0_ref, stage1_ref), cols, w, tm)
        cnt = jnp.minimum(tpos + w // 2, seq_len) - jnp.maximum(tpos - w // 2, 0)
        pooled = (total * (1.0 / cnt.astype(F32)) - z_ref[:, cols]).astype(BF16)
        mixed = jnp.dot(pooled, wpool_ref[g], preferred_element_type=F32)
        mix_ref[:, cols] = (mixed * scale_ref[:, cols]).astype(BF16)
    for cc in col_chunks:
        y_pool = jnp.dot(mix_ref[...], wbp_ref[:, cc], preferred_element_type=F32)
        merged_ref[:, cc] = (o_ref[:, cc] + gp_ref[:, cc].astype(F32) * y_pool).astype(BF16)
    for cc in col_chunks:
        o_ref[:, cc] = h_ref[:, cc] + jnp.dot(merged_ref[...], wout_ref[:, cc], preferred_element_type=F32)


def _merge(attn, z, gates, h, w_pool, pool_scale, w_ba, w_bp, w_out, *, seq_len):
    T, D = h.shape
    C = z.shape[1]
    tm = MERGE_TOKENS
    halo_blocks = tm // POOL_HALO
    last_halo = T // POOL_HALO - 1
    assert T % tm == 0 and seq_len % tm == 0
    return pl.pallas_call(
        functools.partial(_merge_body, seq_len=seq_len),
        out_shape=jax.ShapeDtypeStruct((T, D), F32),
        grid=(T // tm,),
        in_specs=[
            pl.BlockSpec((tm, attn.shape[1]), lambda i: (i, 0)),
            pl.BlockSpec((tm, C), lambda i: (i, 0)),
            pl.BlockSpec((POOL_HALO, C), lambda i: (jnp.maximum(i * halo_blocks - 1, 0), 0)),
            pl.BlockSpec((POOL_HALO, C), lambda i: (jnp.minimum((i + 1) * halo_blocks, last_halo), 0)),
            pl.BlockSpec((tm, D), lambda i: (i, 0)),
            pl.BlockSpec((tm, D), lambda i: (i, 1)),
            pl.BlockSpec((tm, D), lambda i: (i, 0)),
            _resident(w_pool.shape), _resident(pool_scale.shape),
            _resident(w_ba.shape), _resident(w_bp.shape), _resident(w_out.shape),
        ],
        out_specs=pl.BlockSpec((tm, D), lambda i: (i, 0)),
        scratch_shapes=[pltpu.VMEM((tm + 3 * POOL_HALO, C), F32),
                        pltpu.VMEM((tm + 3 * POOL_HALO, C // len(POOL_WINDOWS)), F32),
                        pltpu.VMEM((tm + 3 * POOL_HALO, C // len(POOL_WINDOWS)), F32),
                        pltpu.VMEM((tm, C), BF16), pltpu.VMEM((tm, D), BF16)],
        compiler_params=_params(("parallel",)),
        name="pool_merge_out",
    )(attn, z, z, z, gates, gates, h, w_pool, pool_scale, w_ba, w_bp, w_out)


def _trunk(x, w, *, seq_len):
    B, S, D = x.shape
    x2 = x.reshape(B * S, D)
    h1, u = _ffn(x2, w["g_ffn1"], w["w1_gate"], w["w1_up"], w["w1_down"], w["g_mix"], final=False)
    q, k, v, z, gates = _inproj(u, w["w_in"])
    attn = _attention(q, k, v, w["attn_bias"], img_rows=seq_len // GRID_W)
    h2 = _merge(attn, z, gates, h1, w["w_pool"], w["pool_scale"], w["w_branch_attn"], w["w_branch_pool"],
                w["w_out"], seq_len=seq_len)
    y = _ffn(h2, w["g_ffn2"], w["w2_gate"], w["w2_up"], w["w2_down"], w["g_final"], final=True)
    return y.reshape(B, S, D)


def kernel(x_prompt, x_sample, g_ffn1, w1_gate, w1_up, w1_down, g_mix, w_in, rpb, w_pool, pool_scale,
           w_branch_attn, w_branch_pool, w_out, g_ffn2, w2_gate, w2_up, w2_down, g_final):
    assert g_ffn1.shape[0] == 1, "single-layer block"
    bf = lambda a: a[0].astype(BF16)
    vec = lambda a: a.reshape(1, -1).astype(F32)
    w = {
        "g_ffn1": vec(g_ffn1), "w1_gate": bf(w1_gate), "w1_up": bf(w1_up), "w1_down": bf(w1_down),
        "g_mix": vec(g_mix), "w_in": bf(w_in), "attn_bias": _attn_bias_table(rpb[0]),
        "w_pool": bf(w_pool), "pool_scale": vec(pool_scale),
        "w_branch_attn": bf(w_branch_attn), "w_branch_pool": bf(w_branch_pool), "w_out": bf(w_out),
        "g_ffn2": vec(g_ffn2), "w2_gate": bf(w2_gate), "w2_up": bf(w2_up), "w2_down": bf(w2_down),
        "g_final": vec(g_final),
    }
    y_prompt = _trunk(x_prompt, w, seq_len=x_prompt.shape[1])
    y_sample = _trunk(x_sample, w, seq_len=x_sample.shape[1])
    return (y_prompt, y_sample)
```

```python
import functools

import jax
import jax.numpy as jnp
from jax import lax
from jax.experimental import pallas as pl
from jax.experimental.pallas import tpu as pltpu

F32 = jnp.float32
BF16 = jnp.bfloat16

GRID_W = 64
NA_HEADS = 16
NA_HEAD_DIM = 64
D_ATTN = NA_HEADS * NA_HEAD_DIM
WIN_ROWS = 8
WIN_COLS = 16
POOL_WINDOWS = (2, 4, 8, 16)
POOL_HALO = max(POOL_WINDOWS) // 2
RMS_EPS = 1e-6
MASK_BIAS = -1e30

V7X_LANES = 128
V7X_VMEM_LIMIT_BYTES = 56 * 1024 * 1024

FFN_TOKENS = 1024
FFN_HIDDEN = 512
FFN_LAST_ROWS = 256
PROJ_TOKENS = 1024
PROJ_COLS = 1024
ATTN_ROWS = 8
ATTN_KV_BLOCK_ROWS = 4
ATTN_KV_BLOCKS = 4
ATTN_WIN_ROWS = ATTN_KV_BLOCK_ROWS * ATTN_KV_BLOCKS
MERGE_TOKENS = 512
MERGE_COLS = 512
MERGE_POOL_COLS = 256
MERGE_POOL_ROWS = 128
HEAD_PAIRS = NA_HEADS // 2


def _params(semantics):
    return pltpu.CompilerParams(dimension_semantics=semantics, vmem_limit_bytes=V7X_VMEM_LIMIT_BYTES)


def _resident(shape):
    zeros = (0,) * len(shape)
    return pl.BlockSpec(shape, lambda *_: zeros, pipeline_mode=pl.Buffered(1))


def _rmsnorm(x, g):
    return x * lax.rsqrt(jnp.mean(x * x, axis=-1, keepdims=True) + RMS_EPS) * g


def _sigmoid(x):
    return 1.0 / (1.0 + jnp.exp(-x))


def _ffn_body(x_hbm, gpre_ref, wg_ref, wu_ref, wd_ref, gpost_ref, *rest, final):
    if final:
        acc_ref, xbuf, sem, xn_ref = rest
    else:
        acc_ref, u_ref, xbuf, sem = rest
        xn_ref = u_ref
    i, f = pl.program_id(0), pl.program_id(1)
    n_tiles, n_chunks = pl.num_programs(0), pl.num_programs(1)
    tm = xbuf.shape[0]

    def x_copy(tile):
        return pltpu.make_async_copy(x_hbm.at[pl.ds(tile * tm, tm)], xbuf, sem)

    def tile_step(first, last):
        if first:
            xn = _rmsnorm(xbuf[...], gpre_ref[...]).astype(BF16)
            xn_ref[...] = xn
        else:
            xn = xn_ref[...]
        gate = jnp.dot(xn, wg_ref[...], preferred_element_type=F32)
        up = jnp.dot(xn, wu_ref[...], preferred_element_type=F32)
        hid = (0.5 * gate * _sigmoid(gate) * up).astype(BF16)
        if not last:
            acc_ref[...] = (xbuf[...] if first else acc_ref[...]) + jnp.dot(hid, wd_ref[...],
                                                                          preferred_element_type=F32)
            return
        for r in range(0, tm, FFN_LAST_ROWS):
            rows = slice(r, r + FFN_LAST_ROWS)
            acc = acc_ref[rows, :] + jnp.dot(hid[rows, :], wd_ref[...], preferred_element_type=F32)
            if final:
                acc_ref[rows, :] = _rmsnorm(acc, gpost_ref[...])
            else:
                acc_ref[rows, :] = acc
                u_ref[rows, :] = _rmsnorm(acc, gpost_ref[...]).astype(BF16)

    @pl.when(jnp.logical_and(i == 0, f == 0))
    def _():
        x_copy(0).start()

    @pl.when(f == 0)
    def _():
        x_copy(i).wait()
        tile_step(True, False)

    @pl.when(jnp.logical_and(f == 1, i + 1 < n_tiles))
    def _():
        x_copy(i + 1).start()

    @pl.when(jnp.logical_and(f > 0, f < n_chunks - 1))
    def _():
        tile_step(False, False)

    @pl.when(f == n_chunks - 1)
    def _():
        tile_step(False, True)


def _ffn(x, g_pre, wg, wu, wd, g_post, *, final):
    T, D = x.shape
    Fh = wg.shape[1]
    tm, tf = FFN_TOKENS, FFN_HIDDEN
    assert Fh // tf >= 2 and T % tm == 0 and Fh % tf == 0
    row = pl.BlockSpec((tm, D), lambda i, f: (i, 0))
    vec = pl.BlockSpec((1, D), lambda i, f: (0, 0))
    scratch = [pltpu.VMEM((tm, D), F32), pltpu.SemaphoreType.DMA(())]
    if final:
        out_shape = jax.ShapeDtypeStruct((T, D), F32)
        out_specs = row
        scratch.append(pltpu.VMEM((tm, D), BF16))
    else:
        out_shape = (jax.ShapeDtypeStruct((T, D), F32), jax.ShapeDtypeStruct((T, D), BF16))
        out_specs = (row, row)
    return pl.pallas_call(
        functools.partial(_ffn_body, final=final),
        out_shape=out_shape,
        grid=(T // tm, Fh // tf),
        in_specs=[
            pl.BlockSpec(memory_space=pl.ANY), vec,
            pl.BlockSpec((D, tf), lambda i, f: (0, f)),
            pl.BlockSpec((D, tf), lambda i, f: (0, f)),
            pl.BlockSpec((tf, D), lambda i, f: (f, 0)),
            vec,
        ],
        out_specs=out_specs,
        scratch_shapes=scratch,
        compiler_params=_params(("arbitrary", "arbitrary")),
        name="ffn_final" if final else "ffn_mid",
    )(x, g_pre, wg, wu, wd, g_post)


def _proj_heads_body(u_ref, w_ref, q_ref, k_ref, v_ref, z_ref):
    u = u_ref[...]
    c = q_ref.shape[1]

    def proj(n):
        return jnp.dot(u, w_ref[:, n * c:(n + 1) * c], preferred_element_type=F32)

    q_ref[...] = (proj(0) * (NA_HEAD_DIM ** -0.5)).astype(BF16)
    k_ref[...] = proj(1).astype(BF16)
    v_ref[...] = proj(2).astype(BF16)
    z_ref[...] = proj(3)


def _proj_gates_body(u_ref, w_ref, gate_ref):
    u = u_ref[...]
    for c in range(0, gate_ref.shape[1], PROJ_COLS):
        cols = slice(c, c + PROJ_COLS)
        logits = jnp.dot(u, w_ref[:, cols], preferred_element_type=F32)
        gate_ref[:, cols] = (0.5 * jnp.tanh(0.5 * logits) + 0.5).astype(BF16)


def _inproj(u, w_in):
    T, D = u.shape
    tm, c = PROJ_TOKENS, D_ATTN
    d_heads = 4 * c
    d_gate = w_in.shape[1] - d_heads
    assert d_gate == d_heads, "w_in halves are addressed as two equal column blocks"
    u_spec = pl.BlockSpec((tm, D), lambda i: (i, 0))
    head = pl.BlockSpec((tm, c), lambda i: (i, 0))
    q, k, v, z = pl.pallas_call(
        _proj_heads_body,
        out_shape=(
            jax.ShapeDtypeStruct((T, c), BF16),
            jax.ShapeDtypeStruct((T, c), BF16),
            jax.ShapeDtypeStruct((T, c), BF16),
            jax.ShapeDtypeStruct((T, c), F32),
        ),
        grid=(T // tm,),
        in_specs=[u_spec, pl.BlockSpec((D, d_heads), lambda i: (0, 0), pipeline_mode=pl.Buffered(1))],
        out_specs=(head, head, head, head),
        compiler_params=_params(("parallel",)),
        name="proj_heads",
    )(u, w_in)
    gates = pl.pallas_call(
        _proj_gates_body,
        out_shape=jax.ShapeDtypeStruct((T, d_gate), BF16),
        grid=(T // tm,),
        in_specs=[u_spec, pl.BlockSpec((D, d_gate), lambda i: (0, 1), pipeline_mode=pl.Buffered(1))],
        out_specs=pl.BlockSpec((tm, d_gate), lambda i: (i, 0)),
        compiler_params=_params(("parallel",)),
        name="proj_gates",
    )(u, w_in)
    return q, k, v, z, gates


N_REL_ROWS = 2 * WIN_ROWS - 1
N_REL_COLS = 2 * WIN_COLS - 1


def _bias_table_body(rpb_ref, out_ref):
    c = lax.broadcasted_iota(jnp.int32, (GRID_W, V7X_LANES), 0)
    lane = lax.broadcasted_iota(jnp.int32, (GRID_W, V7X_LANES), 1)
    kc = lane % GRID_W
    cs = jnp.clip(c - WIN_COLS // 2, 0, GRID_W - WIN_COLS)
    valid = (kc >= cs) & (kc < cs + WIN_COLS)
    low = lane < GRID_W
    for a in range(2):
        even, odd = [], []
        for dr in range(N_REL_ROWS):
            x = jnp.broadcast_to(rpb_ref[a, dr:dr + 1, :], (GRID_W, V7X_LANES))
            shift = V7X_LANES - (WIN_COLS - 1)
            even.append(jnp.where(valid, pltpu.roll(x, shift, 1, stride=1, stride_axis=0), MASK_BIAS))
            odd.append(jnp.where(valid, pltpu.roll(x, shift + GRID_W, 1, stride=1, stride_axis=0), MASK_BIAS))
        for off in range(WIN_ROWS):
            for jp in range(WIN_ROWS // 2):
                tile = jnp.where(low, even[off + 2 * jp], odd[off + 2 * jp + 1])
                out_ref[off, 0, a * GRID_W:(a + 1) * GRID_W, jp * V7X_LANES:(jp + 1) * V7X_LANES] = tile


def _attn_bias_table(rpb):
    rpb_lanes = jnp.pad(rpb.astype(F32), ((0, 0), (0, 0), (0, V7X_LANES - N_REL_COLS)))
    return pl.pallas_call(
        _bias_table_body,
        out_shape=jax.ShapeDtypeStruct((WIN_ROWS, HEAD_PAIRS, 2 * GRID_W, WIN_ROWS * GRID_W), F32),
        grid=(HEAD_PAIRS,),
        in_specs=[pl.BlockSpec((2, N_REL_ROWS, V7X_LANES), lambda p: (p, 0, 0))],
        out_specs=pl.BlockSpec((WIN_ROWS, 1, 2 * GRID_W, WIN_ROWS * GRID_W), lambda p: (0, p, 0, 0)),
        compiler_params=_params(("parallel",)),
        name="attn_bias_table",
    )(rpb_lanes)


def _attn_window_start(step, img_rows):
    row0 = step * ATTN_ROWS
    img0 = (row0 // img_rows) * img_rows
    r0 = row0 - img0
    ws = jnp.clip(r0 - WIN_ROWS // 2, 0, img_rows - ATTN_WIN_ROWS)
    return img0 + ws, r0, ws


def _attn_body(q_ref, *rest, img_rows):
    k_blocks = rest[:ATTN_KV_BLOCKS]
    v_blocks = rest[ATTN_KV_BLOCKS:2 * ATTN_KV_BLOCKS]
    bias_ref, o_ref, kw_ref, vw_ref = rest[2 * ATTN_KV_BLOCKS:]
    blk = ATTN_KV_BLOCK_ROWS * GRID_W
    for j in range(ATTN_KV_BLOCKS):
        kw_ref[j * blk:(j + 1) * blk, :] = k_blocks[j][...]
        vw_ref[j * blk:(j + 1) * blk, :] = v_blocks[j][...]

    _, r0, ws = _attn_window_start(pl.program_id(0), img_rows)
    lane = lax.broadcasted_iota(jnp.int32, (GRID_W, V7X_LANES), 1)
    first_head = lane < NA_HEAD_DIM
    band = WIN_ROWS * GRID_W

    def scores(i, p, key0, off):
        lanes = slice(p * V7X_LANES, (p + 1) * V7X_LANES)
        qp = q_ref[i * GRID_W:(i + 1) * GRID_W, lanes]
        zero = jnp.zeros_like(qp)
        lhs = jnp.concatenate([jnp.where(first_head, qp, zero), jnp.where(first_head, zero, qp)], axis=0)
        kb = kw_ref[pl.ds(key0, band), lanes]
        s = lax.dot_general(lhs, kb, (((1,), (1,)), ((), ())), preferred_element_type=F32)
        return s + bias_ref[off, p]

    def row_max(s):
        return jnp.max(s, axis=-1, keepdims=True)

    def softmax(s, m):
        e = jnp.exp(s - m)
        return e.astype(BF16), 1.0 / jnp.sum(e, axis=-1, keepdims=True)

    def values(even, odd):
        i, p, key0, _, _ = even
        vb = vw_ref[pl.ds(key0, band), p * V7X_LANES:(p + 2) * V7X_LANES]
        o = jnp.dot(jnp.concatenate([even[3], odd[3]], axis=0), vb, preferred_element_type=F32)
        for a, (_, pa, _, _, inv) in enumerate((even, odd)):
            oa = o[a * 2 * GRID_W:(a + 1) * 2 * GRID_W, a * V7X_LANES:(a + 1) * V7X_LANES] * inv
            o_ref[i * GRID_W:(i + 1) * GRID_W, pa * V7X_LANES:(pa + 1) * V7X_LANES] = (
                jnp.where(first_head, oa[:GRID_W], oa[GRID_W:]).astype(BF16))

    items = []
    for i in range(ATTN_ROWS):
        r = r0 + i
        rs = jnp.clip(r - WIN_ROWS // 2, 0, img_rows - WIN_ROWS)
        key0 = pl.multiple_of((rs - ws) * GRID_W, GRID_W)
        off = rs - r + (WIN_ROWS - 1)
        items += [(i, p, key0, off) for p in range(HEAD_PAIRS)]
    scored, maxed, normed, held = None, None, None, None
    for n in range(len(items) + 3):
        new_scored = None
        if n < len(items):
            i, p, key0, off = items[n]
            new_scored = (i, p, key0, scores(i, p, key0, off))
        new_maxed = None
        if scored is not None:
            new_maxed = scored + (row_max(scored[3]),)
        new_normed = None
        if maxed is not None:
            i, p, key0, s, m = maxed
            new_normed = (i, p, key0) + softmax(s, m)
        if normed is not None:
            if normed[1] % 2 == 0:
                held = normed
            else:
                values(held, normed)
        scored, maxed, normed = new_scored, new_maxed, new_normed


def _attention(q, k, v, bias, *, img_rows):
    T, C = q.shape
    tq = ATTN_ROWS * GRID_W
    blk = ATTN_KV_BLOCK_ROWS * GRID_W

    def kv_spec(j):
        def index(s):
            win0, _, _ = _attn_window_start(s, img_rows)
            return (win0 // ATTN_KV_BLOCK_ROWS + j, 0)
        return pl.BlockSpec((blk, C), index)

    kv_specs = [kv_spec(j) for j in range(ATTN_KV_BLOCKS)]
    return pl.pallas_call(
        functools.partial(_attn_body, img_rows=img_rows),
        out_shape=jax.ShapeDtypeStruct((T, C), BF16),
        grid=(T // tq,),
        in_specs=[pl.BlockSpec((tq, C), lambda s: (s, 0))] + kv_specs + kv_specs + [_resident(bias.shape)],
        out_specs=pl.BlockSpec((tq, C), lambda s: (s, 0)),
        scratch_shapes=[pltpu.VMEM((ATTN_WIN_ROWS * GRID_W, C), BF16)] * 2,
        compiler_params=_params(("parallel",)),
        name="nbr_attention",
    )(q, *([k] * ATTN_KV_BLOCKS), *([v] * ATTN_KV_BLOCKS), bias)


def _window_sum(ext_ref, stage_refs, cols, w, row0, n_rows):
    rows = n_rows + 2 * POOL_HALO
    src, src_cols, base, span, n_stages = ext_ref, cols, row0, 1, 0
    while 2 * span < w:
        dst = stage_refs[n_stages % 2]
        dst[0:rows, :] = src[base:base + rows, src_cols] + src[base + span:base + span + rows, src_cols]
        src, src_cols, base, span, n_stages = dst, slice(None), 0, 2 * span, n_stages + 1
    lo = base + POOL_HALO - w // 2
    return src[lo:lo + n_rows, src_cols] + src[base + POOL_HALO:base + POOL_HALO + n_rows, src_cols]


def _merge_body(attn_ref, z_ref, zprev_ref, znext_ref, ga_ref, gp_ref, h_ref, wpool_ref, scale_ref,
                wba_ref, wbp_ref, wout_ref, o_ref, ext_ref, stage0_ref, stage1_ref, mix_ref, merged_ref, *, seq_len):
    tm = z_ref.shape[0]
    gd = wpool_ref.shape[1]
    t0 = (pl.program_id(0) * tm) % seq_len
    ext_ref[0:POOL_HALO, :] = jnp.where(t0 == 0, 0.0, zprev_ref[...])
    ext_ref[POOL_HALO:POOL_HALO + tm, :] = z_ref[...]
    ext_ref[POOL_HALO + tm:2 * POOL_HALO + tm, :] = jnp.where(t0 + tm == seq_len, 0.0, znext_ref[...])
    ext_ref[2 * POOL_HALO + tm:, :] = jnp.zeros((POOL_HALO, ext_ref.shape[1]), F32)
    for ref in (stage0_ref, stage1_ref):
        ref[...] = jnp.zeros(ref.shape, F32)
    n_out = o_ref.shape[1]
    half = tm // 2
    piece_rows = MERGE_POOL_ROWS

    def pool_piece(g, row0):
        w = POOL_WINDOWS[g]
        cols = slice(g * gd, (g + 1) * gd)
        rows = slice(row0, row0 + piece_rows)
        total = _window_sum(ext_ref, (stage0_ref, stage1_ref), cols, w, row0, piece_rows)
        tpos = t0 + row0 + lax.broadcasted_iota(jnp.int32, (piece_rows, 1), 0)
        cnt = jnp.minimum(tpos + w // 2, seq_len) - jnp.maximum(tpos - w // 2, 0)
        pooled = (total * (1.0 / cnt.astype(F32)) - z_ref[rows, cols]).astype(BF16)
        mixed = jnp.dot(pooled, wpool_ref[g], preferred_element_type=F32)
        mix_ref[rows, cols] = (mixed * scale_ref[:, cols]).astype(BF16)

    def interleave(matmul_step, pieces):
        chunks = [slice(c, c + MERGE_POOL_COLS) for c in range(0, n_out, MERGE_POOL_COLS)]
        per_chunk = -(-len(pieces) // len(chunks))
        for n, cc in enumerate(chunks):
            matmul_step(cc)
            for piece in pieces[n * per_chunk:(n + 1) * per_chunk]:
                pool_piece(*piece)

    def attn_step(cc):
        y_attn = jnp.dot(attn_ref[...], wba_ref[:, cc], preferred_element_type=F32)
        o_ref[:, cc] = ga_ref[:, cc].astype(F32) * y_attn

    def pool_step(rows):
        def step(cc):
            y_pool = jnp.dot(mix_ref[rows, :], wbp_ref[:, cc], preferred_element_type=F32)
            merged_ref[rows, cc] = (o_ref[rows, cc] + gp_ref[rows, cc].astype(F32) * y_pool).astype(BF16)
        return step

    groups = list(reversed(range(len(POOL_WINDOWS))))
    interleave(attn_step, [(g, row0) for g in groups for row0 in range(0, half, piece_rows)])
    interleave(pool_step(slice(0, half)), [(g, row0) for g in groups for row0 in range(half, tm, piece_rows)])
    interleave(pool_step(slice(half, tm)), [])
    for c in range(0, n_out, MERGE_COLS):
        cc = slice(c, c + MERGE_COLS)
        o_ref[:, cc] = h_ref[:, cc] + jnp.dot(merged_ref[...], wout_ref[:, cc], preferred_element_type=F32)


def _merge(attn, z, gates, h, w_pool, pool_scale, w_ba, w_bp, w_out, *, seq_len):
    T, D = h.shape
    C = z.shape[1]
    tm = MERGE_TOKENS
    halo_blocks = tm // POOL_HALO
    last_halo = T // POOL_HALO - 1
    assert T % tm == 0 and seq_len % tm == 0
    return pl.pallas_call(
        functools.partial(_merge_body, seq_len=seq_len),
        out_shape=jax.ShapeDtypeStruct((T, D), F32),
        grid=(T // tm,),
        in_specs=[
            pl.BlockSpec((tm, attn.shape[1]), lambda i: (i, 0)),
            pl.BlockSpec((tm, C), lambda i: (i, 0)),
            pl.BlockSpec((POOL_HALO, C), lambda i: (jnp.maximum(i * halo_blocks - 1, 0), 0)),
            pl.BlockSpec((POOL_HALO, C), lambda i: (jnp.minimum((i + 1) * halo_blocks, last_halo), 0)),
            pl.BlockSpec((tm, D), lambda i: (i, 0)),
            pl.BlockSpec((tm, D), lambda i: (i, 1)),
            pl.BlockSpec((tm, D), lambda i: (i, 0)),
            _resident(w_pool.shape), _resident(pool_scale.shape),
            _resident(w_ba.shape), _resident(w_bp.shape), _resident(w_out.shape),
        ],
        out_specs=pl.BlockSpec((tm, D), lambda i: (i, 0)),
        scratch_shapes=[pltpu.VMEM((tm + 3 * POOL_HALO, C), F32),
                        pltpu.VMEM((tm + 3 * POOL_HALO, C // len(POOL_WINDOWS)), F32),
                        pltpu.VMEM((tm + 3 * POOL_HALO, C // len(POOL_WINDOWS)), F32),
                        pltpu.VMEM((tm, C), BF16), pltpu.VMEM((tm, D), BF16)],
        compiler_params=_params(("parallel",)),
        name="pool_merge_out",
    )(attn, z, z, z, gates, gates, h, w_pool, pool_scale, w_ba, w_bp, w_out)


def _trunk(x, w, *, seq_len):
    B, S, D = x.shape
    x2 = x.reshape(B * S, D)
    h1, u = _ffn(x2, w["g_ffn1"], w["w1_gate"], w["w1_up"], w["w1_down"], w["g_mix"], final=False)
    q, k, v, z, gates = _inproj(u, w["w_in"])
    attn = _attention(q, k, v, w["attn_bias"], img_rows=seq_len // GRID_W)
    h2 = _merge(attn, z, gates, h1, w["w_pool"], w["pool_scale"], w["w_branch_attn"], w["w_branch_pool"],
                w["w_out"], seq_len=seq_len)
    y = _ffn(h2, w["g_ffn2"], w["w2_gate"], w["w2_up"], w["w2_down"], w["g_final"], final=True)
    return y.reshape(B, S, D)


def kernel(x_prompt, x_sample, g_ffn1, w1_gate, w1_up, w1_down, g_mix, w_in, rpb, w_pool, pool_scale,
           w_branch_attn, w_branch_pool, w_out, g_ffn2, w2_gate, w2_up, w2_down, g_final):
    assert g_ffn1.shape[0] == 1, "single-layer block"
    bf = lambda a: a[0].astype(BF16)
    vec = lambda a: a.reshape(1, -1).astype(F32)
    w = {
        "g_ffn1": vec(g_ffn1), "w1_gate": bf(w1_gate), "w1_up": bf(w1_up), "w1_down": bf(w1_down),
        "g_mix": vec(g_mix), "w_in": bf(w_in), "attn_bias": _attn_bias_table(rpb[0]),
        "w_pool": bf(w_pool), "pool_scale": vec(pool_scale),
        "w_branch_attn": bf(w_branch_attn), "w_branch_pool": bf(w_branch_pool), "w_out": bf(w_out),
        "g_ffn2": vec(g_ffn2), "w2_gate": bf(w2_gate), "w2_up": bf(w2_up), "w2_down": bf(w2_down),
        "g_final": vec(g_final),
    }
    y_prompt = _trunk(x_prompt, w, seq_len=x_prompt.shape[1])
    y_sample = _trunk(x_sample, w, seq_len=x_sample.shape[1])
    return (y_prompt, y_sample)
```

```python
import functools

import jax
import jax.numpy as jnp
from jax import lax
from jax.experimental import pallas as pl
from jax.experimental.pallas import tpu as pltpu

F32 = jnp.float32
BF16 = jnp.bfloat16

GRID_W = 64
NA_HEADS = 16
NA_HEAD_DIM = 64
D_ATTN = NA_HEADS * NA_HEAD_DIM
WIN_ROWS = 8
WIN_COLS = 16
POOL_WINDOWS = (2, 4, 8, 16)
POOL_HALO = max(POOL_WINDOWS) // 2
RMS_EPS = 1e-6
MASK_BIAS = -1e30

V7X_LANES = 128
V7X_VMEM_LIMIT_BYTES = 56 * 1024 * 1024

FFN_TOKENS = 1024
FFN_HIDDEN = 512
PROJ_TOKENS = 1024
PROJ_COLS = 1024
ATTN_ROWS = 8
ATTN_KV_BLOCK_ROWS = 4
ATTN_KV_BLOCKS = 4
ATTN_WIN_ROWS = ATTN_KV_BLOCK_ROWS * ATTN_KV_BLOCKS
MERGE_TOKENS = 512
MERGE_COLS = 512
MERGE_POOL_COLS = 256
MERGE_POOL_ROWS = 128
HEAD_PAIRS = NA_HEADS // 2


def _params(semantics):
    return pltpu.CompilerParams(dimension_semantics=semantics, vmem_limit_bytes=V7X_VMEM_LIMIT_BYTES)


def _resident(shape):
    zeros = (0,) * len(shape)
    return pl.BlockSpec(shape, lambda *_: zeros, pipeline_mode=pl.Buffered(1))


def _rmsnorm(x, g):
    return x * lax.rsqrt(jnp.mean(x * x, axis=-1, keepdims=True) + RMS_EPS) * g


def _sigmoid(x):
    return 1.0 / (1.0 + jnp.exp(-x))


def _ffn_body(x_hbm, gpre_ref, wg_ref, wu_ref, wd_ref, gpost_ref, *rest, final):
    if final:
        acc_ref, xbuf, sem, xn_ref = rest
    else:
        acc_ref, u_ref, xbuf, sem = rest
        xn_ref = u_ref
    i, f = pl.program_id(0), pl.program_id(1)
    n_tiles, n_chunks = pl.num_programs(0), pl.num_programs(1)
    tm = xbuf.shape[0]

    def x_copy(tile):
        return pltpu.make_async_copy(x_hbm.at[pl.ds(tile * tm, tm)], xbuf, sem)

    def tile_step(first, last):
        if first:
            xn = _rmsnorm(xbuf[...], gpre_ref[...]).astype(BF16)
            xn_ref[...] = xn
        else:
            xn = xn_ref[...]
        gate = jnp.dot(xn, wg_ref[...], preferred_element_type=F32)
        up = jnp.dot(xn, wu_ref[...], preferred_element_type=F32)
        hid = (0.5 * gate * _sigmoid(gate) * up).astype(BF16)
        acc = (xbuf[...] if first else acc_ref[...]) + jnp.dot(hid, wd_ref[...], preferred_element_type=F32)
        if not last:
            acc_ref[...] = acc
        elif final:
            acc_ref[...] = _rmsnorm(acc, gpost_ref[...])
        else:
            acc_ref[...] = acc
            u_ref[...] = _rmsnorm(acc, gpost_ref[...]).astype(BF16)

    @pl.when(jnp.logical_and(i == 0, f == 0))
    def _():
        x_copy(0).start()

    @pl.when(f == 0)
    def _():
        x_copy(i).wait()
        tile_step(True, False)

    @pl.when(jnp.logical_and(f == 1, i + 1 < n_tiles))
    def _():
        x_copy(i + 1).start()

    @pl.when(jnp.logical_and(f > 0, f < n_chunks - 1))
    def _():
        tile_step(False, False)

    @pl.when(f == n_chunks - 1)
    def _():
        tile_step(False, True)


def _ffn(x, g_pre, wg, wu, wd, g_post, *, final):
    T, D = x.shape
    Fh = wg.shape[1]
    tm, tf = FFN_TOKENS, FFN_HIDDEN
    assert Fh // tf >= 2 and T % tm == 0 and Fh % tf == 0
    row = pl.BlockSpec((tm, D), lambda i, f: (i, 0))
    vec = pl.BlockSpec((1, D), lambda i, f: (0, 0))
    scratch = [pltpu.VMEM((tm, D), F32), pltpu.SemaphoreType.DMA(())]
    if final:
        out_shape = jax.ShapeDtypeStruct((T, D), F32)
        out_specs = row
        scratch.append(pltpu.VMEM((tm, D), BF16))
    else:
        out_shape = (jax.ShapeDtypeStruct((T, D), F32), jax.ShapeDtypeStruct((T, D), BF16))
        out_specs = (row, row)
    return pl.pallas_call(
        functools.partial(_ffn_body, final=final),
        out_shape=out_shape,
        grid=(T // tm, Fh // tf),
        in_specs=[
            pl.BlockSpec(memory_space=pl.ANY), vec,
            pl.BlockSpec((D, tf), lambda i, f: (0, f)),
            pl.BlockSpec((D, tf), lambda i, f: (0, f)),
            pl.BlockSpec((tf, D), lambda i, f: (f, 0)),
            vec,
        ],
        out_specs=out_specs,
        scratch_shapes=scratch,
        compiler_params=_params(("arbitrary", "arbitrary")),
        name="ffn_final" if final else "ffn_mid",
    )(x, g_pre, wg, wu, wd, g_post)


def _proj_heads_body(u_ref, w_ref, q_ref, k_ref, v_ref, z_ref):
    u = u_ref[...]
    c = q_ref.shape[1]

    def proj(n):
        return jnp.dot(u, w_ref[:, n * c:(n + 1) * c], preferred_element_type=F32)

    q_ref[...] = (proj(0) * (NA_HEAD_DIM ** -0.5)).astype(BF16)
    k_ref[...] = proj(1).astype(BF16)
    v_ref[...] = proj(2).astype(BF16)
    z_ref[...] = proj(3)


def _proj_gates_body(u_ref, w_ref, gate_ref):
    u = u_ref[...]
    for c in range(0, gate_ref.shape[1], PROJ_COLS):
        cols = slice(c, c + PROJ_COLS)
        logits = jnp.dot(u, w_ref[:, cols], preferred_element_type=F32)
        gate_ref[:, cols] = (0.5 * jnp.tanh(0.5 * logits) + 0.5).astype(BF16)


def _inproj(u, w_in):
    T, D = u.shape
    tm, c = PROJ_TOKENS, D_ATTN
    d_heads = 4 * c
    d_gate = w_in.shape[1] - d_heads
    assert d_gate == d_heads, "w_in halves are addressed as two equal column blocks"
    u_spec = pl.BlockSpec((tm, D), lambda i: (i, 0))
    head = pl.BlockSpec((tm, c), lambda i: (i, 0))
    q, k, v, z = pl.pallas_call(
        _proj_heads_body,
        out_shape=(
            jax.ShapeDtypeStruct((T, c), BF16),
            jax.ShapeDtypeStruct((T, c), BF16),
            jax.ShapeDtypeStruct((T, c), BF16),
            jax.ShapeDtypeStruct((T, c), F32),
        ),
        grid=(T // tm,),
        in_specs=[u_spec, pl.BlockSpec((D, d_heads), lambda i: (0, 0), pipeline_mode=pl.Buffered(1))],
        out_specs=(head, head, head, head),
        compiler_params=_params(("parallel",)),
        name="proj_heads",
    )(u, w_in)
    gates = pl.pallas_call(
        _proj_gates_body,
        out_shape=jax.ShapeDtypeStruct((T, d_gate), BF16),
        grid=(T // tm,),
        in_specs=[u_spec, pl.BlockSpec((D, d_gate), lambda i: (0, 1), pipeline_mode=pl.Buffered(1))],
        out_specs=pl.BlockSpec((tm, d_gate), lambda i: (i, 0)),
        compiler_params=_params(("parallel",)),
        name="proj_gates",
    )(u, w_in)
    return q, k, v, z, gates


N_REL_ROWS = 2 * WIN_ROWS - 1
N_REL_COLS = 2 * WIN_COLS - 1


def _bias_table_body(rpb_ref, out_ref):
    c = lax.broadcasted_iota(jnp.int32, (GRID_W, V7X_LANES), 0)
    lane = lax.broadcasted_iota(jnp.int32, (GRID_W, V7X_LANES), 1)
    kc = lane % GRID_W
    cs = jnp.clip(c - WIN_COLS // 2, 0, GRID_W - WIN_COLS)
    valid = (kc >= cs) & (kc < cs + WIN_COLS)
    low = lane < GRID_W
    for a in range(2):
        even, odd = [], []
        for dr in range(N_REL_ROWS):
            x = jnp.broadcast_to(rpb_ref[a, dr:dr + 1, :], (GRID_W, V7X_LANES))
            shift = V7X_LANES - (WIN_COLS - 1)
            even.append(jnp.where(valid, pltpu.roll(x, shift, 1, stride=1, stride_axis=0), MASK_BIAS))
            odd.append(jnp.where(valid, pltpu.roll(x, shift + GRID_W, 1, stride=1, stride_axis=0), MASK_BIAS))
        for off in range(WIN_ROWS):
            for jp in range(WIN_ROWS // 2):
                tile = jnp.where(low, even[off + 2 * jp], odd[off + 2 * jp + 1])
                out_ref[off, 0, a * GRID_W:(a + 1) * GRID_W, jp * V7X_LANES:(jp + 1) * V7X_LANES] = tile


def _attn_bias_table(rpb):
    rpb_lanes = jnp.pad(rpb.astype(F32), ((0, 0), (0, 0), (0, V7X_LANES - N_REL_COLS)))
    return pl.pallas_call(
        _bias_table_body,
        out_shape=jax.ShapeDtypeStruct((WIN_ROWS, HEAD_PAIRS, 2 * GRID_W, WIN_ROWS * GRID_W), F32),
        grid=(HEAD_PAIRS,),
        in_specs=[pl.BlockSpec((2, N_REL_ROWS, V7X_LANES), lambda p: (p, 0, 0))],
        out_specs=pl.BlockSpec((WIN_ROWS, 1, 2 * GRID_W, WIN_ROWS * GRID_W), lambda p: (0, p, 0, 0)),
        compiler_params=_params(("parallel",)),
        name="attn_bias_table",
    )(rpb_lanes)


def _attn_window_start(step, img_rows):
    row0 = step * ATTN_ROWS
    img0 = (row0 // img_rows) * img_rows
    r0 = row0 - img0
    ws = jnp.clip(r0 - WIN_ROWS // 2, 0, img_rows - ATTN_WIN_ROWS)
    return img0 + ws, r0, ws


def _attn_body(q_ref, *rest, img_rows):
    k_blocks = rest[:ATTN_KV_BLOCKS]
    v_blocks = rest[ATTN_KV_BLOCKS:2 * ATTN_KV_BLOCKS]
    bias_ref, o_ref, kw_ref, vw_ref = rest[2 * ATTN_KV_BLOCKS:]
    blk = ATTN_KV_BLOCK_ROWS * GRID_W
    for j in range(ATTN_KV_BLOCKS):
        kw_ref[j * blk:(j + 1) * blk, :] = k_blocks[j][...]
        vw_ref[j * blk:(j + 1) * blk, :] = v_blocks[j][...]

    _, r0, ws = _attn_window_start(pl.program_id(0), img_rows)
    lane = lax.broadcasted_iota(jnp.int32, (GRID_W, V7X_LANES), 1)
    first_head = lane < NA_HEAD_DIM
    band = WIN_ROWS * GRID_W

    def scores(i, p, key0, off):
        lanes = slice(p * V7X_LANES, (p + 1) * V7X_LANES)
        qp = q_ref[i * GRID_W:(i + 1) * GRID_W, lanes]
        zero = jnp.zeros_like(qp)
        lhs = jnp.concatenate([jnp.where(first_head, qp, zero), jnp.where(first_head, zero, qp)], axis=0)
        kb = kw_ref[pl.ds(key0, band), lanes]
        s = lax.dot_general(lhs, kb, (((1,), (1,)), ((), ())), preferred_element_type=F32)
        return s + bias_ref[off, p]

    def row_max(s):
        return jnp.max(s, axis=-1, keepdims=True)

    def softmax(s, m):
        e = jnp.exp(s - m)
        return e.astype(BF16), 1.0 / jnp.sum(e, axis=-1, keepdims=True)

    def values(even, odd):
        i, p, key0, _, _ = even
        vb = vw_ref[pl.ds(key0, band), p * V7X_LANES:(p + 2) * V7X_LANES]
        o = jnp.dot(jnp.concatenate([even[3], odd[3]], axis=0), vb, preferred_element_type=F32)
        for a, (_, pa, _, _, inv) in enumerate((even, odd)):
            oa = o[a * 2 * GRID_W:(a + 1) * 2 * GRID_W, a * V7X_LANES:(a + 1) * V7X_LANES] * inv
            o_ref[i * GRID_W:(i + 1) * GRID_W, pa * V7X_LANES:(pa + 1) * V7X_LANES] = (
                jnp.where(first_head, oa[:GRID_W], oa[GRID_W:]).astype(BF16))

    items = []
    for i in range(ATTN_ROWS):
        r = r0 + i
        rs = jnp.clip(r - WIN_ROWS // 2, 0, img_rows - WIN_ROWS)
        key0 = pl.multiple_of((rs - ws) * GRID_W, GRID_W)
        off = rs - r + (WIN_ROWS - 1)
        items += [(i, p, key0, off) for p in range(HEAD_PAIRS)]
    scored, maxed, normed, held = None, None, None, None
    for n in range(len(items) + 3):
        new_scored = None
        if n < len(items):
            i, p, key0, off = items[n]
            new_scored = (i, p, key0, scores(i, p, key0, off))
        new_maxed = None
        if scored is not None:
            new_maxed = scored + (row_max(scored[3]),)
        new_normed = None
        if maxed is not None:
            i, p, key0, s, m = maxed
            new_normed = (i, p, key0) + softmax(s, m)
        if normed is not None:
            if normed[1] % 2 == 0:
                held = normed
            else:
                values(held, normed)
        scored, maxed, normed = new_scored, new_maxed, new_normed


def _attention(q, k, v, bias, *, img_rows):
    T, C = q.shape
    tq = ATTN_ROWS * GRID_W
    blk = ATTN_KV_BLOCK_ROWS * GRID_W

    def kv_spec(j):
        def index(s):
            win0, _, _ = _attn_window_start(s, img_rows)
            return (win0 // ATTN_KV_BLOCK_ROWS + j, 0)
        return pl.BlockSpec((blk, C), index)

    kv_specs = [kv_spec(j) for j in range(ATTN_KV_BLOCKS)]
    return pl.pallas_call(
        functools.partial(_attn_body, img_rows=img_rows),
        out_shape=jax.ShapeDtypeStruct((T, C), BF16),
        grid=(T // tq,),
        in_specs=[pl.BlockSpec((tq, C), lambda s: (s, 0))] + kv_specs + kv_specs + [_resident(bias.shape)],
        out_specs=pl.BlockSpec((tq, C), lambda s: (s, 0)),
        scratch_shapes=[pltpu.VMEM((ATTN_WIN_ROWS * GRID_W, C), BF16)] * 2,
        compiler_params=_params(("parallel",)),
        name="nbr_attention",
    )(q, *([k] * ATTN_KV_BLOCKS), *([v] * ATTN_KV_BLOCKS), bias)


def _window_sum(ext_ref, stage_refs, cols, w, row0, n_rows):
    rows = n_rows + 2 * POOL_HALO
    src, src_cols, base, span, n_stages = ext_ref, cols, row0, 1, 0
    while 2 * span < w:
        dst = stage_refs[n_stages % 2]
        dst[0:rows, :] = src[base:base + rows, src_cols] + src[base + span:base + span + rows, src_cols]
        src, src_cols, base, span, n_stages = dst, slice(None), 0, 2 * span, n_stages + 1
    lo = base + POOL_HALO - w // 2
    return src[lo:lo + n_rows, src_cols] + src[base + POOL_HALO:base + POOL_HALO + n_rows, src_cols]


def _merge_body(attn_ref, z_ref, zprev_ref, znext_ref, ga_ref, gp_ref, h_ref, wpool_ref, scale_ref,
                wba_ref, wbp_ref, wout_ref, o_ref, ext_ref, stage0_ref, stage1_ref, mix_ref, merged_ref, *, seq_len):
    tm = z_ref.shape[0]
    gd = wpool_ref.shape[1]
    t0 = (pl.program_id(0) * tm) % seq_len
    ext_ref[0:POOL_HALO, :] = jnp.where(t0 == 0, 0.0, zprev_ref[...])
    ext_ref[POOL_HALO:POOL_HALO + tm, :] = z_ref[...]
    ext_ref[POOL_HALO + tm:2 * POOL_HALO + tm, :] = jnp.where(t0 + tm == seq_len, 0.0, znext_ref[...])
    ext_ref[2 * POOL_HALO + tm:, :] = jnp.zeros((POOL_HALO, ext_ref.shape[1]), F32)
    for ref in (stage0_ref, stage1_ref):
        ref[...] = jnp.zeros(ref.shape, F32)
    n_out = o_ref.shape[1]
    half = tm // 2
    piece_rows = MERGE_POOL_ROWS

    def pool_piece(g, row0):
        w = POOL_WINDOWS[g]
        cols = slice(g * gd, (g + 1) * gd)
        rows = slice(row0, row0 + piece_rows)
        total = _window_sum(ext_ref, (stage0_ref, stage1_ref), cols, w, row0, piece_rows)
        tpos = t0 + row0 + lax.broadcasted_iota(jnp.int32, (piece_rows, 1), 0)
        cnt = jnp.minimum(tpos + w // 2, seq_len) - jnp.maximum(tpos - w // 2, 0)
        pooled = (total * (1.0 / cnt.astype(F32)) - z_ref[rows, cols]).astype(BF16)
        mixed = jnp.dot(pooled, wpool_ref[g], preferred_element_type=F32)
        mix_ref[rows, cols] = (mixed * scale_ref[:, cols]).astype(BF16)

    def interleave(matmul_step, pieces):
        chunks = [slice(c, c + MERGE_POOL_COLS) for c in range(0, n_out, MERGE_POOL_COLS)]
        per_chunk = -(-len(pieces) // len(chunks))
        for n, cc in enumerate(chunks):
            matmul_step(cc)
            for piece in pieces[n * per_chunk:(n + 1) * per_chunk]:
                pool_piece(*piece)

    def attn_step(cc):
        y_attn = jnp.dot(attn_ref[...], wba_ref[:, cc], preferred_element_type=F32)
        o_ref[:, cc] = ga_ref[:, cc].astype(F32) * y_attn

    def pool_step(rows):
        def step(cc):
            y_pool = jnp.dot(mix_ref[rows, :], wbp_ref[:, cc], preferred_element_type=F32)
            merged_ref[rows, cc] = (o_ref[rows, cc] + gp_ref[rows, cc].astype(F32) * y_pool).astype(BF16)
        return step

    groups = list(reversed(range(len(POOL_WINDOWS))))
    interleave(attn_step, [(g, row0) for g in groups for row0 in range(0, half, piece_rows)])
    interleave(pool_step(slice(0, half)), [(g, row0) for g in groups for row0 in range(half, tm, piece_rows)])
    interleave(pool_step(slice(half, tm)), [])
    for c in range(0, n_out, MERGE_COLS):
        cc = slice(c, c + MERGE_COLS)
        o_ref[:, cc] = h_ref[:, cc] + jnp.dot(merged_ref[...], wout_ref[:, cc], preferred_element_type=F32)


def _merge(attn, z, gates, h, w_pool, pool_scale, w_ba, w_bp, w_out, *, seq_len):
    T, D = h.shape
    C = z.shape[1]
    tm = MERGE_TOKENS
    halo_blocks = tm // POOL_HALO
    last_halo = T // POOL_HALO - 1
    assert T % tm == 0 and seq_len % tm == 0
    return pl.pallas_call(
        functools.partial(_merge_body, seq_len=seq_len),
        out_shape=jax.ShapeDtypeStruct((T, D), F32),
        grid=(T // tm,),
        in_specs=[
            pl.BlockSpec((tm, attn.shape[1]), lambda i: (i, 0)),
            pl.BlockSpec((tm, C), lambda i: (i, 0)),
            pl.BlockSpec((POOL_HALO, C), lambda i: (jnp.maximum(i * halo_blocks - 1, 0), 0)),
            pl.BlockSpec((POOL_HALO, C), lambda i: (jnp.minimum((i + 1) * halo_blocks, last_halo), 0)),
            pl.BlockSpec((tm, D), lambda i: (i, 0)),
            pl.BlockSpec((tm, D), lambda i: (i, 1)),
            pl.BlockSpec((tm, D), lambda i: (i, 0)),
            _resident(w_pool.shape), _resident(pool_scale.shape),
            _resident(w_ba.shape), _resident(w_bp.shape), _resident(w_out.shape),
        ],
        out_specs=pl.BlockSpec((tm, D), lambda i: (i, 0)),
        scratch_shapes=[pltpu.VMEM((tm + 3 * POOL_HALO, C), F32),
                        pltpu.VMEM((tm + 3 * POOL_HALO, C // len(POOL_WINDOWS)), F32),
                        pltpu.VMEM((tm + 3 * POOL_HALO, C // len(POOL_WINDOWS)), F32),
                        pltpu.VMEM((tm, C), BF16), pltpu.VMEM((tm, D), BF16)],
        compiler_params=_params(("parallel",)),
        name="pool_merge_out",
    )(attn, z, z, z, gates, gates, h, w_pool, pool_scale, w_ba, w_bp, w_out)


def _trunk(x, w, *, seq_len):
    B, S, D = x.shape
    x2 = x.reshape(B * S, D)
    h1, u = _ffn(x2, w["g_ffn1"], w["w1_gate"], w["w1_up"], w["w1_down"], w["g_mix"], final=False)
    q, k, v, z, gates = _inproj(u, w["w_in"])
    attn = _attention(q, k, v, w["attn_bias"], img_rows=seq_len // GRID_W)
    h2 = _merge(attn, z, gates, h1, w["w_pool"], w["pool_scale"], w["w_branch_attn"], w["w_branch_pool"],
                w["w_out"], seq_len=seq_len)
    y = _ffn(h2, w["g_ffn2"], w["w2_gate"], w["w2_up"], w["w2_down"], w["g_final"], final=True)
    return y.reshape(B, S, D)


def kernel(x_prompt, x_sample, g_ffn1, w1_gate, w1_up, w1_down, g_mix, w_in, rpb, w_pool, pool_scale,
           w_branch_attn, w_branch_pool, w_out, g_ffn2, w2_gate, w2_up, w2_down, g_final):
    assert g_ffn1.shape[0] == 1, "single-layer block"
    bf = lambda a: a[0].astype(BF16)
    vec = lambda a: a.reshape(1, -1).astype(F32)
    w = {
        "g_ffn1": vec(g_ffn1), "w1_gate": bf(w1_gate), "w1_up": bf(w1_up), "w1_down": bf(w1_down),
        "g_mix": vec(g_mix), "w_in": bf(w_in), "attn_bias": _attn_bias_table(rpb[0]),
        "w_pool": bf(w_pool), "pool_scale": vec(pool_scale),
        "w_branch_attn": bf(w_branch_attn), "w_branch_pool": bf(w_branch_pool), "w_out": bf(w_out),
        "g_ffn2": vec(g_ffn2), "w2_gate": bf(w2_gate), "w2_up": bf(w2_up), "w2_down": bf(w2_down),
        "g_final": vec(g_final),
    }
    y_prompt = _trunk(x_prompt, w, seq_len=x_prompt.shape[1])
    y_sample = _trunk(x_sample, w, seq_len=x_sample.shape[1])
    return (y_prompt, y_sample)
```

```python
import functools

import jax
import jax.numpy as jnp
from jax import lax
from jax.experimental import pallas as pl
from jax.experimental.pallas import tpu as pltpu

F32 = jnp.float32
BF16 = jnp.bfloat16

GRID_W = 64
NA_HEADS = 16
NA_HEAD_DIM = 64
D_ATTN = NA_HEADS * NA_HEAD_DIM
WIN_ROWS = 8
WIN_COLS = 16
POOL_WINDOWS = (2, 4, 8, 16)
POOL_HALO = max(POOL_WINDOWS) // 2
RMS_EPS = 1e-6
MASK_BIAS = -1e30

V7X_LANES = 128
V7X_VMEM_LIMIT_BYTES = 56 * 1024 * 1024

FFN_TOKENS = 1024
FFN_HIDDEN = 512
PROJ_TOKENS = 1024
PROJ_COLS = 1024
ATTN_ROWS = 16
ATTN_KV_BLOCK_ROWS = 4
ATTN_KV_BLOCKS = 6
ATTN_WIN_ROWS = ATTN_KV_BLOCK_ROWS * ATTN_KV_BLOCKS
MERGE_TOKENS = 512
MERGE_COLS = 512
MERGE_POOL_COLS = 256
MERGE_POOL_ROWS = 128
HEAD_PAIRS = NA_HEADS // 2


def _params(semantics):
    return pltpu.CompilerParams(dimension_semantics=semantics, vmem_limit_bytes=V7X_VMEM_LIMIT_BYTES)


def _resident(shape):
    zeros = (0,) * len(shape)
    return pl.BlockSpec(shape, lambda *_: zeros, pipeline_mode=pl.Buffered(1))


def _rmsnorm(x, g):
    return x * lax.rsqrt(jnp.mean(x * x, axis=-1, keepdims=True) + RMS_EPS) * g


def _sigmoid(x):
    return 1.0 / (1.0 + jnp.exp(-x))


def _ffn_body(x_hbm, gpre_ref, wg_ref, wu_ref, wd_ref, gpost_ref, *rest, final):
    if final:
        acc_ref, xbuf, sem, xn_ref = rest
    else:
        acc_ref, u_ref, xbuf, sem = rest
        xn_ref = u_ref
    i, f = pl.program_id(0), pl.program_id(1)
    n_tiles, n_chunks = pl.num_programs(0), pl.num_programs(1)
    tm = xbuf.shape[0]

    def x_copy(tile):
        return pltpu.make_async_copy(x_hbm.at[pl.ds(tile * tm, tm)], xbuf, sem)

    def tile_step(first, last):
        if first:
            xn = _rmsnorm(xbuf[...], gpre_ref[...]).astype(BF16)
            xn_ref[...] = xn
        else:
            xn = xn_ref[...]
        gate = jnp.dot(xn, wg_ref[...], preferred_element_type=F32)
        up = jnp.dot(xn, wu_ref[...], preferred_element_type=F32)
        hid = (0.5 * gate * _sigmoid(gate) * up).astype(BF16)
        acc = (xbuf[...] if first else acc_ref[...]) + jnp.dot(hid, wd_ref[...], preferred_element_type=F32)
        if not last:
            acc_ref[...] = acc
        elif final:
            acc_ref[...] = _rmsnorm(acc, gpost_ref[...])
        else:
            acc_ref[...] = acc
            u_ref[...] = _rmsnorm(acc, gpost_ref[...]).astype(BF16)

    @pl.when(jnp.logical_and(i == 0, f == 0))
    def _():
        x_copy(0).start()

    @pl.when(f == 0)
    def _():
        x_copy(i).wait()
        tile_step(True, False)

    @pl.when(jnp.logical_and(f == 1, i + 1 < n_tiles))
    def _():
        x_copy(i + 1).start()

    @pl.when(jnp.logical_and(f > 0, f < n_chunks - 1))
    def _():
        tile_step(False, False)

    @pl.when(f == n_chunks - 1)
    def _():
        tile_step(False, True)


def _ffn(x, g_pre, wg, wu, wd, g_post, *, final):
    T, D = x.shape
    Fh = wg.shape[1]
    tm, tf = FFN_TOKENS, FFN_HIDDEN
    assert Fh // tf >= 2 and T % tm == 0 and Fh % tf == 0
    row = pl.BlockSpec((tm, D), lambda i, f: (i, 0))
    vec = pl.BlockSpec((1, D), lambda i, f: (0, 0))
    scratch = [pltpu.VMEM((tm, D), F32), pltpu.SemaphoreType.DMA(())]
    if final:
        out_shape = jax.ShapeDtypeStruct((T, D), F32)
        out_specs = row
        scratch.append(pltpu.VMEM((tm, D), BF16))
    else:
        out_shape = (jax.ShapeDtypeStruct((T, D), F32), jax.ShapeDtypeStruct((T, D), BF16))
        out_specs = (row, row)
    return pl.pallas_call(
        functools.partial(_ffn_body, final=final),
        out_shape=out_shape,
        grid=(T // tm, Fh // tf),
        in_specs=[
            pl.BlockSpec(memory_space=pl.ANY), vec,
            pl.BlockSpec((D, tf), lambda i, f: (0, f)),
            pl.BlockSpec((D, tf), lambda i, f: (0, f)),
            pl.BlockSpec((tf, D), lambda i, f: (f, 0)),
            vec,
        ],
        out_specs=out_specs,
        scratch_shapes=scratch,
        compiler_params=_params(("arbitrary", "arbitrary")),
        name="ffn_final" if final else "ffn_mid",
    )(x, g_pre, wg, wu, wd, g_post)


def _proj_heads_body(u_ref, w_ref, q_ref, k_ref, v_ref, z_ref):
    u = u_ref[...]
    c = q_ref.shape[1]

    def proj(n):
        return jnp.dot(u, w_ref[:, n * c:(n + 1) * c], preferred_element_type=F32)

    q_ref[...] = (proj(0) * (NA_HEAD_DIM ** -0.5)).astype(BF16)
    k_ref[...] = proj(1).astype(BF16)
    v_ref[...] = proj(2).astype(BF16)
    z_ref[...] = proj(3)


def _proj_gates_body(u_ref, w_ref, gate_ref):
    u = u_ref[...]
    for c in range(0, gate_ref.shape[1], PROJ_COLS):
        cols = slice(c, c + PROJ_COLS)
        logits = jnp.dot(u, w_ref[:, cols], preferred_element_type=F32)
        gate_ref[:, cols] = (0.5 * jnp.tanh(0.5 * logits) + 0.5).astype(BF16)


def _inproj(u, w_in):
    T, D = u.shape
    tm, c = PROJ_TOKENS, D_ATTN
    d_heads = 4 * c
    d_gate = w_in.shape[1] - d_heads
    assert d_gate == d_heads, "w_in halves are addressed as two equal column blocks"
    u_spec = pl.BlockSpec((tm, D), lambda i: (i, 0))
    head = pl.BlockSpec((tm, c), lambda i: (i, 0))
    q, k, v, z = pl.pallas_call(
        _proj_heads_body,
        out_shape=(
            jax.ShapeDtypeStruct((T, c), BF16),
            jax.ShapeDtypeStruct((T, c), BF16),
            jax.ShapeDtypeStruct((T, c), BF16),
            jax.ShapeDtypeStruct((T, c), F32),
        ),
        grid=(T // tm,),
        in_specs=[u_spec, pl.BlockSpec((D, d_heads), lambda i: (0, 0), pipeline_mode=pl.Buffered(1))],
        out_specs=(head, head, head, head),
        compiler_params=_params(("parallel",)),
        name="proj_heads",
    )(u, w_in)
    gates = pl.pallas_call(
        _proj_gates_body,
        out_shape=jax.ShapeDtypeStruct((T, d_gate), BF16),
        grid=(T // tm,),
        in_specs=[u_spec, pl.BlockSpec((D, d_gate), lambda i: (0, 1), pipeline_mode=pl.Buffered(1))],
        out_specs=pl.BlockSpec((tm, d_gate), lambda i: (i, 0)),
        compiler_params=_params(("parallel",)),
        name="proj_gates",
    )(u, w_in)
    return q, k, v, z, gates


N_REL_ROWS = 2 * WIN_ROWS - 1
N_REL_COLS = 2 * WIN_COLS - 1


def _bias_table_body(rpb_ref, out_ref):
    c = lax.broadcasted_iota(jnp.int32, (GRID_W, V7X_LANES), 0)
    lane = lax.broadcasted_iota(jnp.int32, (GRID_W, V7X_LANES), 1)
    kc = lane % GRID_W
    cs = jnp.clip(c - WIN_COLS // 2, 0, GRID_W - WIN_COLS)
    valid = (kc >= cs) & (kc < cs + WIN_COLS)
    low = lane < GRID_W
    for a in range(2):
        even, odd = [], []
        for dr in range(N_REL_ROWS):
            x = jnp.broadcast_to(rpb_ref[a, dr:dr + 1, :], (GRID_W, V7X_LANES))
            shift = V7X_LANES - (WIN_COLS - 1)
            even.append(jnp.where(valid, pltpu.roll(x, shift, 1, stride=1, stride_axis=0), MASK_BIAS))
            odd.append(jnp.where(valid, pltpu.roll(x, shift + GRID_W, 1, stride=1, stride_axis=0), MASK_BIAS))
        for off in range(WIN_ROWS):
            for jp in range(WIN_ROWS // 2):
                tile = jnp.where(low, even[off + 2 * jp], odd[off + 2 * jp + 1])
                out_ref[off, 0, a * GRID_W:(a + 1) * GRID_W, jp * V7X_LANES:(jp + 1) * V7X_LANES] = tile


def _attn_bias_table(rpb):
    rpb_lanes = jnp.pad(rpb.astype(F32), ((0, 0), (0, 0), (0, V7X_LANES - N_REL_COLS)))
    return pl.pallas_call(
        _bias_table_body,
        out_shape=jax.ShapeDtypeStruct((WIN_ROWS, HEAD_PAIRS, 2 * GRID_W, WIN_ROWS * GRID_W), F32),
        grid=(HEAD_PAIRS,),
        in_specs=[pl.BlockSpec((2, N_REL_ROWS, V7X_LANES), lambda p: (p, 0, 0))],
        out_specs=pl.BlockSpec((WIN_ROWS, 1, 2 * GRID_W, WIN_ROWS * GRID_W), lambda p: (0, p, 0, 0)),
        compiler_params=_params(("parallel",)),
        name="attn_bias_table",
    )(rpb_lanes)


def _attn_window_start(step, img_rows):
    row0 = step * ATTN_ROWS
    img0 = (row0 // img_rows) * img_rows
    r0 = row0 - img0
    ws = jnp.clip(r0 - WIN_ROWS // 2, 0, img_rows - ATTN_WIN_ROWS)
    return img0 + ws, r0, ws


def _attn_body(q_ref, *rest, img_rows):
    k_blocks = rest[:ATTN_KV_BLOCKS]
    v_blocks = rest[ATTN_KV_BLOCKS:2 * ATTN_KV_BLOCKS]
    bias_ref, o_ref, kw_ref, vw_ref = rest[2 * ATTN_KV_BLOCKS:]
    blk = ATTN_KV_BLOCK_ROWS * GRID_W
    for j in range(ATTN_KV_BLOCKS):
        kw_ref[j * blk:(j + 1) * blk, :] = k_blocks[j][...]
        vw_ref[j * blk:(j + 1) * blk, :] = v_blocks[j][...]

    _, r0, ws = _attn_window_start(pl.program_id(0), img_rows)
    lane = lax.broadcasted_iota(jnp.int32, (GRID_W, V7X_LANES), 1)
    first_head = lane < NA_HEAD_DIM
    band = WIN_ROWS * GRID_W

    def scores(i, p, key0, off):
        lanes = slice(p * V7X_LANES, (p + 1) * V7X_LANES)
        qp = q_ref[i * GRID_W:(i + 1) * GRID_W, lanes]
        zero = jnp.zeros_like(qp)
        lhs = jnp.concatenate([jnp.where(first_head, qp, zero), jnp.where(first_head, zero, qp)], axis=0)
        kb = kw_ref[pl.ds(key0, band), lanes]
        s = lax.dot_general(lhs, kb, (((1,), (1,)), ((), ())), preferred_element_type=F32)
        return s + bias_ref[off, p]

    def row_max(s):
        return jnp.max(s, axis=-1, keepdims=True)

    def softmax(s, m):
        e = jnp.exp(s - m)
        return e.astype(BF16), 1.0 / jnp.sum(e, axis=-1, keepdims=True)

    def values(even, odd):
        i, p, key0, _, _ = even
        vb = vw_ref[pl.ds(key0, band), p * V7X_LANES:(p + 2) * V7X_LANES]
        o = jnp.dot(jnp.concatenate([even[3], odd[3]], axis=0), vb, preferred_element_type=F32)
        for a, (_, pa, _, _, inv) in enumerate((even, odd)):
            oa = o[a * 2 * GRID_W:(a + 1) * 2 * GRID_W, a * V7X_LANES:(a + 1) * V7X_LANES] * inv
            o_ref[i * GRID_W:(i + 1) * GRID_W, pa * V7X_LANES:(pa + 1) * V7X_LANES] = (
                jnp.where(first_head, oa[:GRID_W], oa[GRID_W:]).astype(BF16))

    items = []
    for i in range(ATTN_ROWS):
        r = r0 + i
        rs = jnp.clip(r - WIN_ROWS // 2, 0, img_rows - WIN_ROWS)
        key0 = pl.multiple_of((rs - ws) * GRID_W, GRID_W)
        off = rs - r + (WIN_ROWS - 1)
        items += [(i, p, key0, off) for p in range(HEAD_PAIRS)]
    scored, maxed, normed, held = None, None, None, None
    for n in range(len(items) + 3):
        new_scored = None
        if n < len(items):
            i, p, key0, off = items[n]
            new_scored = (i, p, key0, scores(i, p, key0, off))
        new_maxed = None
        if scored is not None:
            new_maxed = scored + (row_max(scored[3]),)
        new_normed = None
        if maxed is not None:
            i, p, key0, s, m = maxed
            new_normed = (i, p, key0) + softmax(s, m)
        if normed is not None:
            if normed[1] % 2 == 0:
                held = normed
            else:
                values(held, normed)
        scored, maxed, normed = new_scored, new_maxed, new_normed


def _attention(q, k, v, bias, *, img_rows):
    T, C = q.shape
    tq = ATTN_ROWS * GRID_W
    blk = ATTN_KV_BLOCK_ROWS * GRID_W

    def kv_spec(j):
        def index(s):
            win0, _, _ = _attn_window_start(s, img_rows)
            return (win0 // ATTN_KV_BLOCK_ROWS + j, 0)
        return pl.BlockSpec((blk, C), index)

    kv_specs = [kv_spec(j) for j in range(ATTN_KV_BLOCKS)]
    return pl.pallas_call(
        functools.partial(_attn_body, img_rows=img_rows),
        out_shape=jax.ShapeDtypeStruct((T, C), BF16),
        grid=(T // tq,),
        in_specs=[pl.BlockSpec((tq, C), lambda s: (s, 0))] + kv_specs + kv_specs + [_resident(bias.shape)],
        out_specs=pl.BlockSpec((tq, C), lambda s: (s, 0)),
        scratch_shapes=[pltpu.VMEM((ATTN_WIN_ROWS * GRID_W, C), BF16)] * 2,
        compiler_params=_params(("parallel",)),
        name="nbr_attention",
    )(q, *([k] * ATTN_KV_BLOCKS), *([v] * ATTN_KV_BLOCKS), bias)


def _window_sum(ext_ref, stage_refs, cols, w, row0, n_rows):
    rows = n_rows + 2 * POOL_HALO
    src, src_cols, base, span, n_stages = ext_ref, cols, row0, 1, 0
    while 2 * span < w:
        dst = stage_refs[n_stages % 2]
        dst[0:rows, :] = src[base:base + rows, src_cols] + src[base + span:base + span + rows, src_cols]
        src, src_cols, base, span, n_stages = dst, slice(None), 0, 2 * span, n_stages + 1
    lo = base + POOL_HALO - w // 2
    return src[lo:lo + n_rows, src_cols] + src[base + POOL_HALO:base + POOL_HALO + n_rows, src_cols]


def _merge_body(attn_ref, z_ref, zprev_ref, znext_ref, ga_ref, gp_ref, h_ref, wpool_ref, scale_ref,
                wba_ref, wbp_ref, wout_ref, o_ref, ext_ref, stage0_ref, stage1_ref, mix_ref, merged_ref, *, seq_len):
    tm = z_ref.shape[0]
    gd = wpool_ref.shape[1]
    t0 = (pl.program_id(0) * tm) % seq_len
    ext_ref[0:POOL_HALO, :] = jnp.where(t0 == 0, 0.0, zprev_ref[...])
    ext_ref[POOL_HALO:POOL_HALO + tm, :] = z_ref[...]
    ext_ref[POOL_HALO + tm:2 * POOL_HALO + tm, :] = jnp.where(t0 + tm == seq_len, 0.0, znext_ref[...])
    ext_ref[2 * POOL_HALO + tm:, :] = jnp.zeros((POOL_HALO, ext_ref.shape[1]), F32)
    for ref in (stage0_ref, stage1_ref):
        ref[...] = jnp.zeros(ref.shape, F32)
    n_out = o_ref.shape[1]
    half = tm // 2
    piece_rows = MERGE_POOL_ROWS

    def pool_piece(g, row0):
        w = POOL_WINDOWS[g]
        cols = slice(g * gd, (g + 1) * gd)
        rows = slice(row0, row0 + piece_rows)
        total = _window_sum(ext_ref, (stage0_ref, stage1_ref), cols, w, row0, piece_rows)
        tpos = t0 + row0 + lax.broadcasted_iota(jnp.int32, (piece_rows, 1), 0)
        cnt = jnp.minimum(tpos + w // 2, seq_len) - jnp.maximum(tpos - w // 2, 0)
        pooled = (total * (1.0 / cnt.astype(F32)) - z_ref[rows, cols]).astype(BF16)
        mixed = jnp.dot(pooled, wpool_ref[g], preferred_element_type=F32)
        mix_ref[rows, cols] = (mixed * scale_ref[:, cols]).astype(BF16)

    def interleave(matmul_step, pieces):
        chunks = [slice(c, c + MERGE_POOL_COLS) for c in range(0, n_out, MERGE_POOL_COLS)]
        per_chunk = -(-len(pieces) // len(chunks))
        for n, cc in enumerate(chunks):
            matmul_step(cc)
            for piece in pieces[n * per_chunk:(n + 1) * per_chunk]:
                pool_piece(*piece)

    def attn_step(cc):
        y_attn = jnp.dot(attn_ref[...], wba_ref[:, cc], preferred_element_type=F32)
        o_ref[:, cc] = ga_ref[:, cc].astype(F32) * y_attn

    def pool_step(rows):
        def step(cc):
            y_pool = jnp.dot(mix_ref[rows, :], wbp_ref[:, cc], preferred_element_type=F32)
            merged_ref[rows, cc] = (o_ref[rows, cc] + gp_ref[rows, cc].astype(F32) * y_pool).astype(BF16)
        return step

    groups = list(reversed(range(len(POOL_WINDOWS))))
    interleave(attn_step, [(g, row0) for g in groups for row0 in range(0, half, piece_rows)])
    interleave(pool_step(slice(0, half)), [(g, row0) for g in groups for row0 in range(half, tm, piece_rows)])
    interleave(pool_step(slice(half, tm)), [])
    for c in range(0, n_out, MERGE_COLS):
        cc = slice(c, c + MERGE_COLS)
        o_ref[:, cc] = h_ref[:, cc] + jnp.dot(merged_ref[...], wout_ref[:, cc], preferred_element_type=F32)


def _merge(attn, z, gates, h, w_pool, pool_scale, w_ba, w_bp, w_out, *, seq_len):
    T, D = h.shape
    C = z.shape[1]
    tm = MERGE_TOKENS
    halo_blocks = tm // POOL_HALO
    last_halo = T // POOL_HALO - 1
    assert T % tm == 0 and seq_len % tm == 0
    return pl.pallas_call(
        functools.partial(_merge_body, seq_len=seq_len),
        out_shape=jax.ShapeDtypeStruct((T, D), F32),
        grid=(T // tm,),
        in_specs=[
            pl.BlockSpec((tm, attn.shape[1]), lambda i: (i, 0)),
            pl.BlockSpec((tm, C), lambda i: (i, 0)),
            pl.BlockSpec((POOL_HALO, C), lambda i: (jnp.maximum(i * halo_blocks - 1, 0), 0)),
            pl.BlockSpec((POOL_HALO, C), lambda i: (jnp.minimum((i + 1) * halo_blocks, last_halo), 0)),
            pl.BlockSpec((tm, D), lambda i: (i, 0)),
            pl.BlockSpec((tm, D), lambda i: (i, 1)),
            pl.BlockSpec((tm, D), lambda i: (i, 0)),
            _resident(w_pool.shape), _resident(pool_scale.shape),
            _resident(w_ba.shape), _resident(w_bp.shape), _resident(w_out.shape),
        ],
        out_specs=pl.BlockSpec((tm, D), lambda i: (i, 0)),
        scratch_shapes=[pltpu.VMEM((tm + 3 * POOL_HALO, C), F32),
                        pltpu.VMEM((tm + 3 * POOL_HALO, C // len(POOL_WINDOWS)), F32),
                        pltpu.VMEM((tm + 3 * POOL_HALO, C // len(POOL_WINDOWS)), F32),
                        pltpu.VMEM((tm, C), BF16), pltpu.VMEM((tm, D), BF16)],
        compiler_params=_params(("parallel",)),
        name="pool_merge_out",
    )(attn, z, z, z, gates, gates, h, w_pool, pool_scale, w_ba, w_bp, w_out)


def _trunk(x, w, *, seq_len):
    B, S, D = x.shape
    x2 = x.reshape(B * S, D)
    h1, u = _ffn(x2, w["g_ffn1"], w["w1_gate"], w["w1_up"], w["w1_down"], w["g_mix"], final=False)
    q, k, v, z, gates = _inproj(u, w["w_in"])
    attn = _attention(q, k, v, w["attn_bias"], img_rows=seq_len // GRID_W)
    h2 = _merge(attn, z, gates, h1, w["w_pool"], w["pool_scale"], w["w_branch_attn"], w["w_branch_pool"],
                w["w_out"], seq_len=seq_len)
    y = _ffn(h2, w["g_ffn2"], w["w2_gate"], w["w2_up"], w["w2_down"], w["g_final"], final=True)
    return y.reshape(B, S, D)


def kernel(x_prompt, x_sample, g_ffn1, w1_gate, w1_up, w1_down, g_mix, w_in, rpb, w_pool, pool_scale,
           w_branch_attn, w_branch_pool, w_out, g_ffn2, w2_gate, w2_up, w2_down, g_final):
    assert g_ffn1.shape[0] == 1, "single-layer block"
    bf = lambda a: a[0].astype(BF16)
    vec = lambda a: a.reshape(1, -1).astype(F32)
    w = {
        "g_ffn1": vec(g_ffn1), "w1_gate": bf(w1_gate), "w1_up": bf(w1_up), "w1_down": bf(w1_down),
        "g_mix": vec(g_mix), "w_in": bf(w_in), "attn_bias": _attn_bias_table(rpb[0]),
        "w_pool": bf(w_pool), "pool_scale": vec(pool_scale),
        "w_branch_attn": bf(w_branch_attn), "w_branch_pool": bf(w_branch_pool), "w_out": bf(w_out),
        "g_ffn2": vec(g_ffn2), "w2_gate": bf(w2_gate), "w2_up": bf(w2_up), "w2_down": bf(w2_down),
        "g_final": vec(g_final),
    }
    y_prompt = _trunk(x_prompt, w, seq_len=x_prompt.shape[1])
    y_sample = _trunk(x_sample, w, seq_len=x_sample.shape[1])
    return (y_prompt, y_sample)
```
